```python
import math
import jax, jax.numpy as jnp
from jax import lax
import numpy as np

D_MODEL = 1024
BATCH = 8
SEQ = 4096
DEPTH = 4

N_MIXERS = 3
D_FF = 4 * D_MODEL
D_MIX = D_MODEL
EPS = 1e-6
CONV_WIDTH = 3
S5_GROUP = 16
S5_GROUPS = D_MIX // S5_GROUP
S5_STATE = 64
DT_MIN = 1e-3
DT_MAX = 1e-1
CHUNK = 128
SG_HEADS = 8
SG_HEAD_DIM = D_MIX // SG_HEADS
N_A = (DEPTH + 2) // 3
N_B = (DEPTH + 1) // 3
N_C = DEPTH // 3

kernel_name = "hybrid_conv_s5_sgmlp_trunk"


def rmsnorm(x, g):
    xf = x.astype(jnp.float32)
    y = xf * lax.rsqrt(jnp.mean(xf * xf, axis=-1, keepdims=True) + EPS)
    return (y * g.astype(jnp.float32)).astype(x.dtype)


def short_conv_mixer(h, w_in, conv_w, conv_b, w_out):
    bcx = h @ w_in
    b_gate, c_gate, xh = jnp.split(bcx, 3, axis=-1)
    z = c_gate * xh
    conv = lax.conv_general_dilated(
        z, conv_w[:, None, :].astype(z.dtype), window_strides=(1,),
        padding=[(CONV_WIDTH - 1, 0)], dimension_numbers=("NWC", "WIO", "NWC"),
        feature_group_count=D_MIX) + conv_b
    return (b_gate * conv) @ w_out


def _ssm_combine(e1, e2):
    a1r, a1i, b1r, b1i = e1
    a2r, a2i, b2r, b2i = e2
    ar = a2r * a1r - a2i * a1i
    ai = a2r * a1i + a2i * a1r
    br = a2r * b1r - a2i * b1i + b2r
    bi = a2r * b1i + a2i * b1r + b2i
    return (ar, ai, br, bi)


def s5_mixer(h, w_in, a_re, a_im, log_dt, b_re, b_im, c_re, c_im, d_skip, glu_w, glu_b, w_out):
    bsz, seq_len, _ = h.shape
    f32 = jnp.float32
    u = (h @ w_in).astype(f32).reshape(bsz, seq_len, S5_GROUPS, S5_GROUP)
    a_re = a_re.astype(f32); a_im = a_im.astype(f32)
    dt = jnp.exp(log_dt.astype(f32))[:, None]
    mag = jnp.exp(a_re * dt)
    abar_re = mag * jnp.cos(a_im * dt)
    abar_im = mag * jnp.sin(a_im * dt)
    den = a_re * a_re + a_im * a_im
    nr = abar_re - 1.0
    ni = abar_im
    f_re = ((nr * a_re + ni * a_im) / den)[..., None]
    f_im = ((ni * a_re - nr * a_im) / den)[..., None]
    b_re = b_re.astype(f32); b_im = b_im.astype(f32)
    bbar_re = f_re * b_re - f_im * b_im
    bbar_im = f_re * b_im + f_im * b_re
    bu_re = jnp.einsum("blgh,gph->blgp", u, bbar_re)
    bu_im = jnp.einsum("blgh,gph->blgp", u, bbar_im)
    a_seq_re = jnp.broadcast_to(abar_re, (1, seq_len, S5_GROUPS, S5_STATE))
    a_seq_im = jnp.broadcast_to(abar_im, (1, seq_len, S5_GROUPS, S5_STATE))
    _, _, s_re, s_im = lax.associative_scan(
        _ssm_combine, (a_seq_re, a_seq_im, bu_re, bu_im), axis=1)
    y = (jnp.einsum("blgp,ghp->blgh", s_re, c_re.astype(f32))
         - jnp.einsum("blgp,ghp->blgh", s_im, c_im.astype(f32)))
    y = y + d_skip.astype(f32).reshape(S5_GROUPS, S5_GROUP) * u
    y = jax.nn.gelu(y.reshape(bsz, seq_len, D_MIX))
    y = y * jax.nn.sigmoid(y @ glu_w.astype(f32) + glu_b.astype(f32))
    return y.astype(h.dtype) @ w_out


def spatial_gating_mixer(h, w_in, v_gain, w_s, b_s, w_out):
    bsz, seq_len, _ = h.shape
    u, v = jnp.split(h @ w_in, 2, axis=-1)
    v = rmsnorm(v, v_gain)
    vc = v.reshape(bsz, seq_len // CHUNK, CHUNK, SG_HEADS, SG_HEAD_DIM)
    causal = jnp.tril(jnp.ones((CHUNK, CHUNK), dtype=bool))
    ws = jnp.where(causal[None], w_s, jnp.zeros_like(w_s))
    vm = jnp.einsum("hts,bnshd->bnthd", ws, vc) + b_s.T[:, :, None]
    return (u * vm.reshape(bsz, seq_len, D_MIX)) @ w_out


def squared_relu_mlp(h, w1, w2):
    return jnp.square(jax.nn.relu(h @ w1)) @ w2


def setup_inputs(seed: int = 0) -> dict:
    key = jax.random.key(seed)
    ks = iter(jax.random.split(key, 40))
    f32 = jnp.float32

    def nrm(shape, std):
        return std * jax.random.normal(next(ks), shape, f32)

    D = D_MODEL
    G, P = S5_GROUPS, S5_STATE
    x = nrm((BATCH, SEQ, D), 1.0)
    c = nrm((BATCH, D), 1.0)
    ada_w = nrm((DEPTH, D, 6 * D), 0.5 * D ** -0.5)
    ada_b = nrm((DEPTH, 6 * D), 0.02)
    norm1_g = 1.0 + nrm((DEPTH, D), 0.02)
    norm2_g = 1.0 + nrm((DEPTH, D), 0.02)
    ff_w1 = nrm((DEPTH, D, D_FF), D ** -0.5)
    ff_w2 = nrm((DEPTH, D_FF, D), D_FF ** -0.5)
    final_g = 1.0 + nrm((D,), 0.02)
    conv_w_in = nrm((N_A, D, 3 * D_MIX), D ** -0.5)
    conv_w = nrm((N_A, CONV_WIDTH, D_MIX), CONV_WIDTH ** -0.5)
    conv_b = nrm((N_A, D_MIX), 0.02)
    conv_w_out = nrm((N_A, D_MIX, D), D_MIX ** -0.5)
    ssm_w_in = nrm((N_B, D, D_MIX), D ** -0.5)
    ssm_a_re = -0.5 + nrm((N_B, G, P), 0.01)
    ssm_a_im = math.pi * jnp.arange(P, dtype=f32) + nrm((N_B, G, P), 0.01)
    ssm_log_dt = jax.random.uniform(next(ks), (N_B, G), f32,
                                    minval=math.log(DT_MIN), maxval=math.log(DT_MAX))
    ssm_b_re = nrm((N_B, G, P, S5_GROUP), S5_GROUP ** -0.5)
    ssm_b_im = nrm((N_B, G, P, S5_GROUP), S5_GROUP ** -0.5)
    ssm_c_re = nrm((N_B, G, S5_GROUP, P), P ** -0.5)
    ssm_c_im = nrm((N_B, G, S5_GROUP, P), P ** -0.5)
    ssm_d = nrm((N_B, D_MIX), 0.5)
    ssm_glu_w = nrm((N_B, D_MIX, D_MIX), D_MIX ** -0.5)
    ssm_glu_b = nrm((N_B, D_MIX), 0.02)
    ssm_w_out = nrm((N_B, D_MIX, D), D_MIX ** -0.5)
    sg_w_in = nrm((N_C, D, 2 * D_MIX), D ** -0.5)
    sg_v_g = 1.0 + nrm((N_C, D_MIX), 0.02)
    sg_w_s = nrm((N_C, SG_HEADS, CHUNK, CHUNK), CHUNK ** -0.5)
    sg_b_s = 1.0 + nrm((N_C, SG_HEADS, CHUNK), 0.02)
    sg_w_out = nrm((N_C, D_MIX, D), D_MIX ** -0.5)
    return {
        "x": x, "c": c, "ada_w": ada_w, "ada_b": ada_b,
        "norm1_g": norm1_g, "norm2_g": norm2_g, "ff_w1": ff_w1, "ff_w2": ff_w2,
        "final_g": final_g,
        "conv_w_in": conv_w_in, "conv_w": conv_w, "conv_b": conv_b, "conv_w_out": conv_w_out,
        "ssm_w_in": ssm_w_in, "ssm_a_re": ssm_a_re, "ssm_a_im": ssm_a_im,
        "ssm_log_dt": ssm_log_dt, "ssm_b_re": ssm_b_re, "ssm_b_im": ssm_b_im,
        "ssm_c_re": ssm_c_re, "ssm_c_im": ssm_c_im, "ssm_d": ssm_d,
        "ssm_glu_w": ssm_glu_w, "ssm_glu_b": ssm_glu_b, "ssm_w_out": ssm_w_out,
        "sg_w_in": sg_w_in, "sg_v_g": sg_v_g, "sg_w_s": sg_w_s, "sg_b_s": sg_b_s,
        "sg_w_out": sg_w_out,
    }


def reference(x, c, ada_w, ada_b, norm1_g, norm2_g, ff_w1, ff_w2, final_g,
              conv_w_in, conv_w, conv_b, conv_w_out,
              ssm_w_in, ssm_a_re, ssm_a_im, ssm_log_dt, ssm_b_re, ssm_b_im,
              ssm_c_re, ssm_c_im, ssm_d, ssm_glu_w, ssm_glu_b, ssm_w_out,
              sg_w_in, sg_v_g, sg_w_s, sg_b_s, sg_w_out):
    c_act = jax.nn.silu(c)
    for i in range(DEPTH):
        kind = i % N_MIXERS
        j = i // N_MIXERS
        mod = (c_act @ ada_w[i] + ada_b[i])[:, None, :]
        sh1, sc1, g1, sh2, sc2, g2 = jnp.split(mod, 6, axis=-1)
        h = rmsnorm(x, norm1_g[i]) * (1.0 + sc1) + sh1
        if kind == 0:
            y = short_conv_mixer(h, conv_w_in[j], conv_w[j], conv_b[j], conv_w_out[j])
        elif kind == 1:
            y = s5_mixer(h, ssm_w_in[j], ssm_a_re[j], ssm_a_im[j], ssm_log_dt[j],
                         ssm_b_re[j], ssm_b_im[j], ssm_c_re[j], ssm_c_im[j], ssm_d[j],
                         ssm_glu_w[j], ssm_glu_b[j], ssm_w_out[j])
        else:
            y = spatial_gating_mixer(h, sg_w_in[j], sg_v_g[j], sg_w_s[j], sg_b_s[j], sg_w_out[j])
        x = x + g1 * y
        h = rmsnorm(x, norm2_g[i]) * (1.0 + sc2) + sh2
        x = x + g2 * squared_relu_mlp(h, ff_w1[i], ff_w2[i])
    return rmsnorm(x, final_g)
```

```python
import functools

import jax
import jax.numpy as jnp
from jax import lax
from jax.experimental import pallas as pl
from jax.experimental.pallas import tpu as pltpu

D_MODEL = 1024
BATCH = 8
SEQ = 4096
DEPTH = 4
N_MIXERS = 3
D_FF = 4 * D_MODEL
D_MIX = D_MODEL
EPS = 1e-6
CONV_WIDTH = 3
S5_GROUP = 16
S5_GROUPS = D_MIX // S5_GROUP
S5_STATE = 64
CHUNK = 128
SG_HEADS = 8
SG_HEAD_DIM = D_MIX // SG_HEADS

SUBLANES = 8
LANES = 128
MXU_DIM = 256

S5_SUPER = MXU_DIM // S5_GROUP
S5_NSUPER = S5_GROUPS // S5_SUPER
S5_SUPER_STATE = S5_SUPER * S5_STATE
S5_NSTATE = S5_GROUPS * S5_STATE

ROW_TILE = 512
S5_STEPS = ROW_TILE // BATCH
FF_TILE = 1024
SCAN_COLS = 512
SCAN_UNROLL = 8
VMEM_LIMIT = 60 * 1024 * 1024

F32 = jnp.float32
BF16 = jnp.bfloat16


def _dot(a, b):
    return jnp.dot(a, b, preferred_element_type=F32)


def _rmsnorm(x, g):
    y = x * lax.rsqrt(jnp.mean(x * x, axis=-1, keepdims=True) + EPS)
    return y * g


def _const_spec(shape):
    zeros = (0,) * len(shape)
    return pl.BlockSpec(shape, lambda *_: zeros)


def _params(*semantics):
    return pltpu.CompilerParams(dimension_semantics=semantics, vmem_limit_bytes=VMEM_LIMIT)


def _adaln_kernel(c_ref, w_ref, b_ref, o_ref):
    c = c_ref[...]
    c_act = (c * jax.nn.sigmoid(c)).astype(BF16)
    o_ref[...] = _dot(c_act, w_ref[...].astype(BF16)) + b_ref[...]


def _adaln(c, ada_w, ada_b):
    d = D_MODEL
    return pl.pallas_call(
        _adaln_kernel,
        grid=(DEPTH, 6),
        in_specs=[
            _const_spec((BATCH, d)),
            pl.BlockSpec((None, d, d), lambda i, j: (i, 0, j)),
            pl.BlockSpec((None, 1, d), lambda i, j: (i, 0, j)),
        ],
        out_specs=pl.BlockSpec((None, BATCH, d), lambda i, j: (i, 0, j)),
        out_shape=jax.ShapeDtypeStruct((DEPTH, BATCH, 6 * d), F32),
        compiler_params=_params("arbitrary", "arbitrary"),
        name="adaln",
    )(c, ada_w, ada_b.reshape(DEPTH, 1, 6 * d))


def _batch_major_spec():
    return pl.BlockSpec((None, ROW_TILE, D_MODEL), lambda b, l: (b, l, 0))


def _time_major_spec():
    return pl.BlockSpec((ROW_TILE, D_MODEL), lambda b, l: (l, b))


def _mod_spec():
    return pl.BlockSpec((None, 6, D_MODEL), lambda b, l: (b, 0, 0))


def _mlp_kernel(x_ref, mod_ref, g_ref, w1_ref, w2_ref, *rest, final):
    if final:
        fg_ref, o_ref = rest
    else:
        (o_ref,) = rest
    x = x_ref[...]
    h = (_rmsnorm(x, g_ref[...]) * (1.0 + mod_ref[4:5, :]) + mod_ref[3:4, :]).astype(BF16)
    acc = jnp.zeros((ROW_TILE, D_MODEL), F32)
    for k in range(D_FF // FF_TILE):
        hid = _dot(h, w1_ref[:, k * FF_TILE:(k + 1) * FF_TILE])
        hid = jnp.square(jnp.maximum(hid, 0.0)).astype(BF16)
        acc = acc + _dot(hid, w2_ref[k * FF_TILE:(k + 1) * FF_TILE, :])
    out = x + mod_ref[5:6, :] * acc
    if final:
        out = _rmsnorm(out, fg_ref[...])
    o_ref[...] = out


def _mlp_layer(x, mod, g, w1, w2, final_g, *, time_major_in, time_major_out):
    d = D_MODEL
    in_spec = _time_major_spec() if time_major_in else _batch_major_spec()
    if time_major_out:
        out_spec, out_shape = _time_major_spec(), (SEQ, BATCH * d)
    else:
        out_spec, out_shape = _batch_major_spec(), (BATCH, SEQ, d)
    final = final_g is not None
    in_specs = [in_spec, _mod_spec(), _const_spec((1, d)),
                _const_spec((d, D_FF)), _const_spec((D_FF, d))]
    args = [x, mod, g.reshape(1, d), w1.astype(BF16), w2.astype(BF16)]
    if final:
        in_specs.append(_const_spec((1, d)))
        args.append(final_g.reshape(1, d))
    return pl.pallas_call(
        functools.partial(_mlp_kernel, final=final),
        grid=(BATCH, SEQ // ROW_TILE),
        in_specs=in_specs,
        out_specs=out_spec,
        out_shape=jax.ShapeDtypeStruct(out_shape, F32),
        compiler_params=_params("parallel", "arbitrary"),
        name="mlp",
    )(*args)


def _conv_kernel(x_ref, mod_ref, g_ref, win_ref, cw_ref, cb_ref, wout_ref, o_ref, zbuf):
    d = D_MIX
    halo = SUBLANES

    @pl.when(pl.program_id(1) == 0)
    def _():
        zbuf[0:halo, :] = jnp.zeros((halo, d), F32)

    x = x_ref[...]
    h = (_rmsnorm(x, g_ref[...]) * (1.0 + mod_ref[1:2, :]) + mod_ref[0:1, :]).astype(BF16)
    bcx = _dot(h, win_ref[...])
    b_gate = bcx[:, 0:d]
    z = bcx[:, d:2 * d] * bcx[:, 2 * d:3 * d]
    zbuf[halo:halo + ROW_TILE, :] = z
    conv = (cw_ref[2:3, :] * z
            + cw_ref[1:2, :] * zbuf[halo - 1:halo - 1 + ROW_TILE, :]
            + cw_ref[0:1, :] * zbuf[halo - 2:halo - 2 + ROW_TILE, :]
            + cb_ref[...])
    zbuf[0:halo, :] = zbuf[ROW_TILE:ROW_TILE + halo, :]
    y = _dot((b_gate * conv).astype(BF16), wout_ref[...])
    o_ref[...] = x + mod_ref[2:3, :] * y


def _conv_layer(x, mod, g, w_in, cw, cb, w_out):
    d = D_MODEL
    return pl.pallas_call(
        _conv_kernel,
        grid=(BATCH, SEQ // ROW_TILE),
        in_specs=[_batch_major_spec(), _mod_spec(), _const_spec((1, d)),
                  _const_spec((d, 3 * D_MIX)), _const_spec((CONV_WIDTH, D_MIX)),
                  _const_spec((1, D_MIX)), _const_spec((D_MIX, d))],
        out_specs=_batch_major_spec(),
        out_shape=jax.ShapeDtypeStruct((BATCH, SEQ, d), F32),
        scratch_shapes=[pltpu.VMEM((ROW_TILE + SUBLANES, D_MIX), F32)],
        compiler_params=_params("parallel", "arbitrary"),
        name="conv_mixer",
    )(x, mod, g.reshape(1, d), w_in.astype(BF16), cw, cb.reshape(1, D_MIX), w_out.astype(BF16))


def _sg_kernel(x_ref, mod_ref, g_ref, win_ref, vg_ref, ws_ref, bs_ref, wout_ref, o_ref, vm_ref):
    d = D_MIX
    x = x_ref[...]
    h = (_rmsnorm(x, g_ref[...]) * (1.0 + mod_ref[1:2, :]) + mod_ref[0:1, :]).astype(BF16)
    uv = _dot(h, win_ref[...])
    u = uv[:, 0:d]
    v = _rmsnorm(uv[:, d:2 * d], vg_ref[...]).astype(BF16)
    row = lax.broadcasted_iota(jnp.int32, (CHUNK, CHUNK), 0)
    col = lax.broadcasted_iota(jnp.int32, (CHUNK, CHUNK), 1)
    causal = col <= row
    for hd in range(SG_HEADS):
        ws = jnp.where(causal, ws_ref[hd], 0.0).astype(BF16)
        bias = bs_ref[:, hd:hd + 1]
        lanes = slice(hd * SG_HEAD_DIM, (hd + 1) * SG_HEAD_DIM)
        for n in range(ROW_TILE // CHUNK):
            rows = slice(n * CHUNK, (n + 1) * CHUNK)
            vm_ref[rows, lanes] = _dot(ws, v[rows, lanes]) + bias
    y = _dot((u * vm_ref[...]).astype(BF16), wout_ref[...])
    o_ref[...] = x + mod_ref[2:3, :] * y


def _sg_layer(x, mod, g, w_in, v_g, w_s, b_s, w_out):
    d = D_MODEL
    return pl.pallas_call(
        _sg_kernel,
        grid=(BATCH, SEQ // ROW_TILE),
        in_specs=[_batch_major_spec(), _mod_spec(), _const_spec((1, d)),
                  _const_spec((d, 2 * D_MIX)), _const_spec((1, D_MIX)),
                  _const_spec((SG_HEADS, CHUNK, CHUNK)), _const_spec((CHUNK, SG_HEADS)),
                  _const_spec((D_MIX, d))],
        out_specs=_batch_major_spec(),
        out_shape=jax.ShapeDtypeStruct((BATCH, SEQ, d), F32),
        scratch_shapes=[pltpu.VMEM((ROW_TILE, D_MIX), F32)],
        compiler_params=_params("parallel", "arbitrary"),
        name="sg_mixer",
    )(x, mod, g.reshape(1, d), w_in.astype(BF16), v_g.reshape(1, D_MIX), w_s, b_s.T,
      w_out.astype(BF16))


def _s5_prep_kernel(are_ref, aim_ref, ldt_ref, bre_ref, bim_ref,
                    abar_re_ref, abar_im_ref, bbar_re_ref, bbar_im_ref):
    a_re = are_ref[...]
    a_im = aim_ref[...]
    dt = jnp.exp(ldt_ref[...])
    mag = jnp.exp(a_re * dt)
    abar_re = mag * jnp.cos(a_im * dt)
    abar_im = mag * jnp.sin(a_im * dt)
    den = a_re * a_re + a_im * a_im
    nr = abar_re - 1.0
    ni = abar_im
    f_re = ((nr * a_re + ni * a_im) / den)[:, None, :]
    f_im = ((ni * a_re - nr * a_im) / den)[:, None, :]
    b_re = bre_ref[...]
    b_im = bim_ref[...]
    abar_re_ref[...] = abar_re
    abar_im_ref[...] = abar_im
    bbar_re_ref[...] = f_re * b_re - f_im * b_im
    bbar_im_ref[...] = f_re * b_im + f_im * b_re


def _s5_prep(a_re, a_im, log_dt, b_re, b_im):
    g, p, h = S5_GROUPS, S5_STATE, S5_GROUP
    return pl.pallas_call(
        _s5_prep_kernel,
        out_shape=[jax.ShapeDtypeStruct((g, p), F32), jax.ShapeDtypeStruct((g, p), F32),
                   jax.ShapeDtypeStruct((g, h, p), F32), jax.ShapeDtypeStruct((g, h, p), F32)],
        name="s5_prep",
    )(a_re, a_im, log_dt.reshape(g, 1), b_re.transpose(0, 2, 1), b_im.transpose(0, 2, 1))


def _block_diag_in(w):
    w = w.reshape(S5_NSUPER, S5_SUPER, S5_GROUP, S5_STATE)
    eye = jnp.eye(S5_SUPER, dtype=w.dtype)
    out = w[:, :, :, None, :] * eye[None, :, None, :, None]
    return out.reshape(S5_NSUPER, MXU_DIM, S5_SUPER_STATE)


def _block_diag_out(w):
    return _block_diag_in(w).transpose(0, 2, 1)


def _s5_kernel(x_ref, mod_ref, g_ref, win_ref, are_ref, aim_ref, wbre_ref, wbim_ref,
               wcre_ref, wcim_ref, dskip_ref, gluw_ref, glub_ref, wout_ref, o_ref,
               sre_ref, sim_ref, state_re, state_im, y_ref):
    d = D_MODEL

    @pl.when(pl.program_id(0) == 0)
    def _():
        state_re[...] = jnp.zeros_like(state_re)
        state_im[...] = jnp.zeros_like(state_im)

    x = x_ref[...]
    xn = _rmsnorm(x, g_ref[...]).reshape(S5_STEPS, BATCH, d)
    h = (xn * (1.0 + mod_ref[1]) + mod_ref[0]).reshape(ROW_TILE, d).astype(BF16)
    u = _dot(h, win_ref[...])
    ub = u.astype(BF16)

    for k in range(S5_NSUPER):
        uk = ub[:, k * MXU_DIM:(k + 1) * MXU_DIM]
        sre_ref[...] = _dot(uk, wbre_ref[k])
        sim_ref[...] = _dot(uk, wbim_ref[k])
        for c0 in range(0, S5_SUPER_STATE, SCAN_COLS):
            cols = slice(c0, c0 + SCAN_COLS)
            gcols = slice(k * S5_SUPER_STATE + c0, k * S5_SUPER_STATE + c0 + SCAN_COLS)
            a_re = jnp.broadcast_to(are_ref[:, gcols], (BATCH, SCAN_COLS))
            a_im = jnp.broadcast_to(aim_ref[:, gcols], (BATCH, SCAN_COLS))

            def step(t, carry, cols=cols, a_re=a_re, a_im=a_im):
                s_re, s_im = carry
                rows = pl.ds(pl.multiple_of(t * BATCH, BATCH), BATCH)
                n_re = a_re * s_re - a_im * s_im + sre_ref[rows, cols]
                n_im = a_re * s_im + a_im * s_re + sim_ref[rows, cols]
                sre_ref[rows, cols] = n_re
                sim_ref[rows, cols] = n_im
                return n_re, n_im

            s_re, s_im = lax.fori_loop(0, S5_STEPS, step,
                                       (state_re[:, gcols], state_im[:, gcols]),
                                       unroll=SCAN_UNROLL)
            state_re[:, gcols] = s_re
            state_im[:, gcols] = s_im
        y_ref[:, k * MXU_DIM:(k + 1) * MXU_DIM] = (
            _dot(sre_ref[...].astype(BF16), wcre_ref[k])
            - _dot(sim_ref[...].astype(BF16), wcim_ref[k]))

    y = jax.nn.gelu(y_ref[...] + dskip_ref[...] * u)
    y = y * jax.nn.sigmoid(_dot(y.astype(BF16), gluw_ref[...]) + glub_ref[...])
    out = _dot(y.astype(BF16), wout_ref[...]).reshape(S5_STEPS, BATCH, d)
    o_ref[...] = x + (mod_ref[2] * out).reshape(ROW_TILE, d)


def _s5_layer(x_tm, mod, g, w_in, a_re, a_im, log_dt, b_re, b_im, c_re, c_im, d_skip,
              glu_w, glu_b, w_out):
    d = D_MODEL
    abar_re, abar_im, bbar_re, bbar_im = _s5_prep(a_re, a_im, log_dt, b_re, b_im)
    wb_re = _block_diag_in(bbar_re).astype(BF16)
    wb_im = _block_diag_in(bbar_im).astype(BF16)
    wc_re = _block_diag_out(c_re).astype(BF16)
    wc_im = _block_diag_out(c_im).astype(BF16)
    row_spec = pl.BlockSpec((ROW_TILE, d), lambda i: (i, 0))
    return pl.pallas_call(
        _s5_kernel,
        grid=(SEQ * BATCH // ROW_TILE,),
        in_specs=[row_spec, _const_spec((6, BATCH, d)), _const_spec((1, d)),
                  _const_spec((d, D_MIX)),
                  _const_spec((1, S5_NSTATE)), _const_spec((1, S5_NSTATE)),
                  _const_spec((S5_NSUPER, MXU_DIM, S5_SUPER_STATE)),
                  _const_spec((S5_NSUPER, MXU_DIM, S5_SUPER_STATE)),
                  _const_spec((S5_NSUPER, S5_SUPER_STATE, MXU_DIM)),
                  _const_spec((S5_NSUPER, S5_SUPER_STATE, MXU_DIM)),
                  _const_spec((1, D_MIX)), _const_spec((D_MIX, D_MIX)),
                  _const_spec((1, D_MIX)), _const_spec((D_MIX, d))],
        out_specs=row_spec,
        out_shape=jax.ShapeDtypeStruct((SEQ * BATCH, d), F32),
        scratch_shapes=[pltpu.VMEM((ROW_TILE, S5_SUPER_STATE), F32),
                        pltpu.VMEM((ROW_TILE, S5_SUPER_STATE), F32),
                        pltpu.VMEM((BATCH, S5_NSTATE), F32),
                        pltpu.VMEM((BATCH, S5_NSTATE), F32),
                        pltpu.VMEM((ROW_TILE, D_MIX), F32)],
        compiler_params=_params("arbitrary"),
        name="s5_mixer",
    )(x_tm, mod.transpose(1, 0, 2), g.reshape(1, d), w_in.astype(BF16),
      abar_re.reshape(1, S5_NSTATE), abar_im.reshape(1, S5_NSTATE),
      wb_re, wb_im, wc_re, wc_im, d_skip.reshape(1, D_MIX),
      glu_w.astype(BF16), glu_b.reshape(1, D_MIX), w_out.astype(BF16))


def kernel(x, c, ada_w, ada_b, norm1_g, norm2_g, ff_w1, ff_w2, final_g, conv_w_in, conv_w, conv_b, conv_w_out, ssm_w_in, ssm_a_re, ssm_a_im, ssm_log_dt, ssm_b_re, ssm_b_im, ssm_c_re, ssm_c_im, ssm_d, ssm_glu_w, ssm_glu_b, ssm_w_out, sg_w_in, sg_v_g, sg_w_s, sg_b_s, sg_w_out):
    d = D_MODEL
    mods = _adaln(c, ada_w, ada_b).reshape(DEPTH, BATCH, 6, d)
    time_major = False
    for i in range(DEPTH):
        kind = i % N_MIXERS
        j = i // N_MIXERS
        next_is_s5 = i + 1 < DEPTH and (i + 1) % N_MIXERS == 1
        if kind == 0:
            x = _conv_layer(x, mods[i], norm1_g[i], conv_w_in[j], conv_w[j], conv_b[j],
                            conv_w_out[j])
        elif kind == 1:
            assert time_major
            x = _s5_layer(x.reshape(SEQ * BATCH, d), mods[i], norm1_g[i], ssm_w_in[j],
                          ssm_a_re[j], ssm_a_im[j], ssm_log_dt[j], ssm_b_re[j], ssm_b_im[j],
                          ssm_c_re[j], ssm_c_im[j], ssm_d[j], ssm_glu_w[j], ssm_glu_b[j],
                          ssm_w_out[j]).reshape(SEQ, BATCH * d)
        else:
            x = _sg_layer(x, mods[i], norm1_g[i], sg_w_in[j], sg_v_g[j], sg_w_s[j], sg_b_s[j],
                          sg_w_out[j])
        x = _mlp_layer(x, mods[i], norm2_g[i], ff_w1[i], ff_w2[i],
                       final_g if i == DEPTH - 1 else None,
                       time_major_in=time_major, time_major_out=next_is_s5)
        time_major = next_is_s5
    return x
```

```python
import functools

import jax
import jax.numpy as jnp
from jax import lax
from jax.experimental import pallas as pl
from jax.experimental.pallas import tpu as pltpu

D_MODEL = 1024
BATCH = 8
SEQ = 4096
DEPTH = 4
N_MIXERS = 3
D_FF = 4 * D_MODEL
D_MIX = D_MODEL
EPS = 1e-6
CONV_WIDTH = 3
S5_GROUP = 16
S5_GROUPS = D_MIX // S5_GROUP
S5_STATE = 64
CHUNK = 128
SG_HEADS = 8
SG_HEAD_DIM = D_MIX // SG_HEADS

SUBLANES = 8
LANES = 128
MXU_DIM = 256

S5_SUPER = MXU_DIM // S5_GROUP
S5_NSUPER = S5_GROUPS // S5_SUPER
S5_SUPER_STATE = S5_SUPER * S5_STATE
S5_NSTATE = S5_GROUPS * S5_STATE

ROW_TILE = 512
S5_STEPS = ROW_TILE // BATCH
S5_PERM_STEPS = MXU_DIM // BATCH
FF_TILE = 1024
SCAN_COLS = 512
SCAN_UNROLL = 8
VMEM_LIMIT = 60 * 1024 * 1024

SHIFT1, SCALE1, GATE1, SHIFT2, SCALE2, GATE2 = range(6)

F32 = jnp.float32
BF16 = jnp.bfloat16


def _dot(a, b):
    return jnp.dot(a, b, preferred_element_type=F32)


def _rmsnorm(x, g):
    y = x * lax.rsqrt(jnp.mean(x * x, axis=-1, keepdims=True) + EPS)
    return y * g


def _const_spec(shape):
    zeros = (0,) * len(shape)
    return pl.BlockSpec(shape, lambda *_: zeros)


def _layer_spec(shape, layer):
    zeros = (0,) * len(shape)
    return pl.BlockSpec((None,) + tuple(shape), lambda *_: (layer,) + zeros)


def _params(*semantics):
    return pltpu.CompilerParams(dimension_semantics=semantics, vmem_limit_bytes=VMEM_LIMIT)


def _adaln_kernel(c_ref, w_ref, b_ref, o_ref):
    c = c_ref[...]
    c_act = (c * jax.nn.sigmoid(c)).astype(BF16)
    o_ref[...] = _dot(c_act, w_ref[...].astype(BF16)) + b_ref[...]


def _adaln(c, ada_w, ada_b):
    d = D_MODEL
    return pl.pallas_call(
        _adaln_kernel,
        grid=(DEPTH, 6),
        in_specs=[
            _const_spec((BATCH, d)),
            pl.BlockSpec((None, d, d), lambda i, j: (i, 0, j)),
            pl.BlockSpec((None, 1, d), lambda i, j: (i, 0, j)),
        ],
        out_specs=pl.BlockSpec((None, None, BATCH, d), lambda i, j: (i, j, 0, 0)),
        out_shape=jax.ShapeDtypeStruct((DEPTH, 6, BATCH, d), F32),
        compiler_params=_params("arbitrary", "arbitrary"),
        name="adaln",
    )(c, ada_w, ada_b.reshape(DEPTH, 1, 6 * d))


def _row_spec():
    return pl.BlockSpec((None, ROW_TILE, D_MODEL), lambda b, l: (b, l, 0))


def _mod_spec(layer):
    return _layer_spec((6, BATCH, D_MODEL), layer)


def _mod_row(mod_ref, which):
    return mod_ref[which, pl.ds(pl.program_id(0), 1), :]


def _mlp_kernel(x_ref, mod_ref, g_ref, w1_ref, w2_ref, fg_ref, o_ref, *, layer, final):
    x = x_ref[...]
    h = _rmsnorm(x, g_ref[layer:layer + 1, :]) * (1.0 + _mod_row(mod_ref, SCALE2))
    h = (h + _mod_row(mod_ref, SHIFT2)).astype(BF16)
    acc = jnp.zeros((ROW_TILE, D_MODEL), F32)
    for k in range(D_FF // FF_TILE):
        hid = _dot(h, w1_ref[:, k * FF_TILE:(k + 1) * FF_TILE])
        hid = jnp.square(jnp.maximum(hid, 0.0)).astype(BF16)
        acc = acc + _dot(hid, w2_ref[k * FF_TILE:(k + 1) * FF_TILE, :])
    out = x + _mod_row(mod_ref, GATE2) * acc
    if final:
        out = _rmsnorm(out, fg_ref[...])
    o_ref[...] = out


def _mlp_layer(x, mods, norm_g, w1, w2, final_g, *, layer):
    d = D_MODEL
    return pl.pallas_call(
        functools.partial(_mlp_kernel, layer=layer, final=layer == DEPTH - 1),
        grid=(BATCH, SEQ // ROW_TILE),
        in_specs=[_row_spec(), _mod_spec(layer), _const_spec((DEPTH, d)),
                  _layer_spec((d, D_FF), layer), _layer_spec((D_FF, d), layer),
                  _const_spec((1, d))],
        out_specs=_row_spec(),
        out_shape=jax.ShapeDtypeStruct((BATCH, SEQ, d), F32),
        compiler_params=_params("parallel", "arbitrary"),
        name="mlp",
    )(x, mods, norm_g, w1, w2, final_g)


def _conv_kernel(x_ref, mod_ref, g_ref, win_ref, cw_ref, cb_ref, wout_ref, o_ref, zbuf, *,
                 layer, idx):
    d = D_MIX
    halo = SUBLANES

    @pl.when(pl.program_id(1) == 0)
    def _():
        zbuf[0:halo, :] = jnp.zeros((halo, d), F32)

    x = x_ref[...]
    h = _rmsnorm(x, g_ref[layer:layer + 1, :]) * (1.0 + _mod_row(mod_ref, SCALE1))
    h = (h + _mod_row(mod_ref, SHIFT1)).astype(BF16)
    bcx = _dot(h, win_ref[...])
    b_gate = bcx[:, 0:d]
    z = bcx[:, d:2 * d] * bcx[:, 2 * d:3 * d]
    zbuf[halo:halo + ROW_TILE, :] = z
    conv = (cw_ref[idx, 2:3, :] * z
            + cw_ref[idx, 1:2, :] * zbuf[halo - 1:halo - 1 + ROW_TILE, :]
            + cw_ref[idx, 0:1, :] * zbuf[halo - 2:halo - 2 + ROW_TILE, :]
            + cb_ref[idx:idx + 1, :])
    zbuf[0:halo, :] = zbuf[ROW_TILE:ROW_TILE + halo, :]
    y = _dot((b_gate * conv).astype(BF16), wout_ref[...])
    o_ref[...] = x + _mod_row(mod_ref, GATE1) * y


def _conv_layer(x, mods, norm_g, w_in, cw, cb, w_out, *, layer, idx):
    d = D_MODEL
    n = cw.shape[0]
    return pl.pallas_call(
        functools.partial(_conv_kernel, layer=layer, idx=idx),
        grid=(BATCH, SEQ // ROW_TILE),
        in_specs=[_row_spec(), _mod_spec(layer), _const_spec((DEPTH, d)),
                  _layer_spec((d, 3 * D_MIX), idx), _const_spec((n, CONV_WIDTH, D_MIX)),
                  _const_spec((n, D_MIX)), _layer_spec((D_MIX, d), idx)],
        out_specs=_row_spec(),
        out_shape=jax.ShapeDtypeStruct((BATCH, SEQ, d), F32),
        scratch_shapes=[pltpu.VMEM((ROW_TILE + SUBLANES, D_MIX), F32)],
        compiler_params=_params("parallel", "arbitrary"),
        name="conv_mixer",
    )(x, mods, norm_g, w_in, cw, cb, w_out)


def _sg_kernel(x_ref, mod_ref, g_ref, win_ref, vg_ref, ws_ref, bs_ref, wout_ref, o_ref, vm_ref, *,
               layer, idx):
    d = D_MIX
    x = x_ref[...]
    h = _rmsnorm(x, g_ref[layer:layer + 1, :]) * (1.0 + _mod_row(mod_ref, SCALE1))
    h = (h + _mod_row(mod_ref, SHIFT1)).astype(BF16)
    uv = _dot(h, win_ref[...])
    u = uv[:, 0:d]
    v = _rmsnorm(uv[:, d:2 * d], vg_ref[idx:idx + 1, :]).astype(BF16)
    row = lax.broadcasted_iota(jnp.int32, (CHUNK, CHUNK), 0)
    col = lax.broadcasted_iota(jnp.int32, (CHUNK, CHUNK), 1)
    causal = col <= row
    for hd in range(SG_HEADS):
        ws = jnp.where(causal, ws_ref[hd], 0.0).astype(BF16)
        bias = bs_ref[:, hd:hd + 1]
        lanes = slice(hd * SG_HEAD_DIM, (hd + 1) * SG_HEAD_DIM)
        for n in range(ROW_TILE // CHUNK):
            rows = slice(n * CHUNK, (n + 1) * CHUNK)
            vm_ref[rows, lanes] = _dot(ws, v[rows, lanes]) + bias
    y = _dot((u * vm_ref[...]).astype(BF16), wout_ref[...])
    o_ref[...] = x + _mod_row(mod_ref, GATE1) * y


def _sg_layer(x, mods, norm_g, w_in, v_g, w_s, b_s_t, w_out, *, layer, idx):
    d = D_MODEL
    n = v_g.shape[0]
    return pl.pallas_call(
        functools.partial(_sg_kernel, layer=layer, idx=idx),
        grid=(BATCH, SEQ // ROW_TILE),
        in_specs=[_row_spec(), _mod_spec(layer), _const_spec((DEPTH, d)),
                  _layer_spec((d, 2 * D_MIX), idx), _const_spec((n, D_MIX)),
                  _layer_spec((SG_HEADS, CHUNK, CHUNK), idx), _layer_spec((CHUNK, SG_HEADS), idx),
                  _layer_spec((D_MIX, d), idx)],
        out_specs=_row_spec(),
        out_shape=jax.ShapeDtypeStruct((BATCH, SEQ, d), F32),
        scratch_shapes=[pltpu.VMEM((ROW_TILE, D_MIX), F32)],
        compiler_params=_params("parallel", "arbitrary"),
        name="sg_mixer",
    )(x, mods, norm_g, w_in, v_g, w_s, b_s_t, w_out)


def _s5_prep_kernel(are_ref, aim_ref, ldt_ref, bre_ref, bim_ref,
                    abar_re_ref, abar_im_ref, bbar_re_ref, bbar_im_ref):
    a_re = are_ref[...]
    a_im = aim_ref[...]
    dt = jnp.exp(ldt_ref[...])
    mag = jnp.exp(a_re * dt)
    abar_re = mag * jnp.cos(a_im * dt)
    abar_im = mag * jnp.sin(a_im * dt)
    den = a_re * a_re + a_im * a_im
    nr = abar_re - 1.0
    ni = abar_im
    f_re = ((nr * a_re + ni * a_im) / den)[:, None, :]
    f_im = ((ni * a_re - nr * a_im) / den)[:, None, :]
    b_re = bre_ref[...]
    b_im = bim_ref[...]
    abar_re_ref[...] = abar_re
    abar_im_ref[...] = abar_im
    bbar_re_ref[...] = f_re * b_re - f_im * b_im
    bbar_im_ref[...] = f_re * b_im + f_im * b_re


def _s5_prep(a_re, a_im, log_dt, b_re, b_im):
    g, p, h = S5_GROUPS, S5_STATE, S5_GROUP
    return pl.pallas_call(
        _s5_prep_kernel,
        out_shape=[jax.ShapeDtypeStruct((g, p), F32), jax.ShapeDtypeStruct((g, p), F32),
                   jax.ShapeDtypeStruct((g, h, p), F32), jax.ShapeDtypeStruct((g, h, p), F32)],
        name="s5_prep",
    )(a_re, a_im, log_dt.reshape(g, 1), b_re.transpose(0, 2, 1), b_im.transpose(0, 2, 1))


def _block_diag_in(w):
    w = w.reshape(S5_NSUPER, S5_SUPER, S5_GROUP, S5_STATE)
    eye = jnp.eye(S5_SUPER, dtype=w.dtype)
    out = w[:, :, :, None, :] * eye[None, :, None, :, None]
    return out.reshape(S5_NSUPER, MXU_DIM, S5_SUPER_STATE)


def _block_diag_out(w):
    return _block_diag_in(w).transpose(0, 2, 1)


def _time_major_perm():
    r = lax.broadcasted_iota(jnp.int32, (MXU_DIM, MXU_DIM), 0)
    c = lax.broadcasted_iota(jnp.int32, (MXU_DIM, MXU_DIM), 1)
    return r, c, (r % BATCH) * S5_PERM_STEPS + r // BATCH


def _s5_kernel(x_ref, mod_ref, g_ref, win_ref, are_ref, aim_ref, wbre_ref, wbim_ref,
               wcre_ref, wcim_ref, dskip_ref, gluw_ref, glub_ref, wout_ref, o_ref,
               h_ref, sre_ref, sim_ref, state_re, state_im, y_ref, *, layer, idx):
    d = D_MODEL

    @pl.when(pl.program_id(0) == 0)
    def _():
        state_re[...] = jnp.zeros_like(state_re)
        state_im[...] = jnp.zeros_like(state_im)

    r, c, src = _time_major_perm()
    to_time_major = jnp.where(c == src, 1.0, 0.0).astype(BF16)
    to_batch_major = jnp.where(r == (c % BATCH) * S5_PERM_STEPS + c // BATCH, 1.0, 0.0).astype(BF16)

    g = g_ref[layer:layer + 1, :]
    for part in range(S5_STEPS // S5_PERM_STEPS):
        steps = slice(part * S5_PERM_STEPS, (part + 1) * S5_PERM_STEPS)
        h_rows = []
        for b in range(BATCH):
            h_b = _rmsnorm(x_ref[b, steps, :], g) * (1.0 + mod_ref[SCALE1, b:b + 1, :])
            h_rows.append((h_b + mod_ref[SHIFT1, b:b + 1, :]).astype(BF16))
        h_part = jnp.concatenate(h_rows, axis=0)
        h_ref[part * MXU_DIM:(part + 1) * MXU_DIM, :] = _dot(to_time_major, h_part).astype(BF16)

    u = _dot(h_ref[...], win_ref[...])
    ub = u.astype(BF16)
    for k in range(S5_NSUPER):
        uk = ub[:, k * MXU_DIM:(k + 1) * MXU_DIM]
        sre_ref[...] = _dot(uk, wbre_ref[k])
        sim_ref[...] = _dot(uk, wbim_ref[k])
        for c0 in range(0, S5_SUPER_STATE, SCAN_COLS):
            cols = slice(c0, c0 + SCAN_COLS)
            gcols = slice(k * S5_SUPER_STATE + c0, k * S5_SUPER_STATE + c0 + SCAN_COLS)
            a_re = jnp.broadcast_to(are_ref[:, gcols], (BATCH, SCAN_COLS))
            a_im = jnp.broadcast_to(aim_ref[:, gcols], (BATCH, SCAN_COLS))

            def step(t, carry, cols=cols, a_re=a_re, a_im=a_im):
                s_re, s_im = carry
                rows = pl.ds(pl.multiple_of(t * BATCH, BATCH), BATCH)
                n_re = a_re * s_re - a_im * s_im + sre_ref[rows, cols]
                n_im = a_re * s_im + a_im * s_re + sim_ref[rows, cols]
                sre_ref[rows, cols] = n_re
                sim_ref[rows, cols] = n_im
                return n_re, n_im

            s_re, s_im = lax.fori_loop(0, S5_STEPS, step,
                                       (state_re[:, gcols], state_im[:, gcols]),
                                       unroll=SCAN_UNROLL)
            state_re[:, gcols] = s_re
            state_im[:, gcols] = s_im
        y_ref[:, k * MXU_DIM:(k + 1) * MXU_DIM] = (
            _dot(sre_ref[...].astype(BF16), wcre_ref[k])
            - _dot(sim_ref[...].astype(BF16), wcim_ref[k]))

    y = jax.nn.gelu(y_ref[...] + dskip_ref[idx:idx + 1, :] * u)
    y = y * jax.nn.sigmoid(_dot(y.astype(BF16), gluw_ref[...]) + glub_ref[idx:idx + 1, :])
    yb = y.astype(BF16)

    for part in range(S5_STEPS // S5_PERM_STEPS):
        steps = slice(part * S5_PERM_STEPS, (part + 1) * S5_PERM_STEPS)
        y_part = _dot(to_batch_major, yb[part * MXU_DIM:(part + 1) * MXU_DIM, :]).astype(BF16)
        out = _dot(y_part, wout_ref[...])
        for b in range(BATCH):
            out_b = out[b * S5_PERM_STEPS:(b + 1) * S5_PERM_STEPS, :]
            o_ref[b, steps, :] = x_ref[b, steps, :] + mod_ref[GATE1, b:b + 1, :] * out_b


def _s5_layer(x, mods, norm_g, w_in, abar_re, abar_im, wb_re, wb_im, wc_re, wc_im, d_skip,
              glu_w, glu_b, w_out, *, layer, idx):
    d = D_MODEL
    n = d_skip.shape[0]
    x_spec = pl.BlockSpec((BATCH, S5_STEPS, d), lambda i: (0, i, 0))
    return pl.pallas_call(
        functools.partial(_s5_kernel, layer=layer, idx=idx),
        grid=(SEQ // S5_STEPS,),
        in_specs=[x_spec, _mod_spec(layer), _const_spec((DEPTH, d)),
                  _layer_spec((d, D_MIX), idx),
                  _const_spec((1, S5_NSTATE)), _const_spec((1, S5_NSTATE)),
                  _const_spec((S5_NSUPER, MXU_DIM, S5_SUPER_STATE)),
                  _const_spec((S5_NSUPER, MXU_DIM, S5_SUPER_STATE)),
                  _const_spec((S5_NSUPER, S5_SUPER_STATE, MXU_DIM)),
                  _const_spec((S5_NSUPER, S5_SUPER_STATE, MXU_DIM)),
                  _const_spec((n, D_MIX)), _layer_spec((D_MIX, D_MIX), idx),
                  _const_spec((n, D_MIX)), _layer_spec((D_MIX, d), idx)],
        out_specs=x_spec,
        out_shape=jax.ShapeDtypeStruct((BATCH, SEQ, d), F32),
        scratch_shapes=[pltpu.VMEM((ROW_TILE, d), BF16),
                        pltpu.VMEM((ROW_TILE, S5_SUPER_STATE), F32),
                        pltpu.VMEM((ROW_TILE, S5_SUPER_STATE), F32),
                        pltpu.VMEM((BATCH, S5_NSTATE), F32),
                        pltpu.VMEM((BATCH, S5_NSTATE), F32),
                        pltpu.VMEM((ROW_TILE, D_MIX), F32)],
        compiler_params=_params("arbitrary"),
        name="s5_mixer",
    )(x, mods, norm_g, w_in, abar_re, abar_im, wb_re, wb_im, wc_re, wc_im, d_skip,
      glu_w, glu_b, w_out)


def kernel(x, c, ada_w, ada_b, norm1_g, norm2_g, ff_w1, ff_w2, final_g, conv_w_in, conv_w, conv_b, conv_w_out, ssm_w_in, ssm_a_re, ssm_a_im, ssm_log_dt, ssm_b_re, ssm_b_im, ssm_c_re, ssm_c_im, ssm_d, ssm_glu_w, ssm_glu_b, ssm_w_out, sg_w_in, sg_v_g, sg_w_s, sg_b_s, sg_w_out):
    mods = _adaln(c, ada_w, ada_b)
    bf = lambda w: w.astype(BF16)
    ff_w1, ff_w2 = bf(ff_w1), bf(ff_w2)
    conv_w_in, conv_w_out = bf(conv_w_in), bf(conv_w_out)
    ssm_w_in, ssm_glu_w, ssm_w_out = bf(ssm_w_in), bf(ssm_glu_w), bf(ssm_w_out)
    sg_w_in, sg_w_out = bf(sg_w_in), bf(sg_w_out)
    sg_b_s_t = sg_b_s.transpose(0, 2, 1)
    final_g = final_g.reshape(1, D_MODEL)
    for i in range(DEPTH):
        kind = i % N_MIXERS
        j = i // N_MIXERS
        if kind == 0:
            x = _conv_layer(x, mods, norm1_g, conv_w_in, conv_w, conv_b, conv_w_out,
                            layer=i, idx=j)
        elif kind == 1:
            abar_re, abar_im, bbar_re, bbar_im = _s5_prep(
                ssm_a_re[j], ssm_a_im[j], ssm_log_dt[j], ssm_b_re[j], ssm_b_im[j])
            x = _s5_layer(x, mods, norm1_g, ssm_w_in,
                          abar_re.reshape(1, S5_NSTATE), abar_im.reshape(1, S5_NSTATE),
                          bf(_block_diag_in(bbar_re)), bf(_block_diag_in(bbar_im)),
                          bf(_block_diag_out(ssm_c_re[j])), bf(_block_diag_out(ssm_c_im[j])),
                          ssm_d, ssm_glu_w, ssm_glu_b, ssm_w_out, layer=i, idx=j)
        else:
            x = _sg_layer(x, mods, norm1_g, sg_w_in, sg_v_g, sg_w_s, sg_b_s_t, sg_w_out,
                          layer=i, idx=j)
        x = _mlp_layer(x, mods, norm2_g, ff_w1, ff_w2, final_g, layer=i)
    return x
```

```python
import functools

import jax
import jax.numpy as jnp
from jax import lax
from jax.experimental import pallas as pl
from jax.experimental.pallas import tpu as pltpu

D_MODEL = 1024
BATCH = 8
SEQ = 4096
DEPTH = 4
N_MIXERS = 3
D_FF = 4 * D_MODEL
D_MIX = D_MODEL
EPS = 1e-6
CONV_WIDTH = 3
S5_GROUP = 16
S5_GROUPS = D_MIX // S5_GROUP
S5_STATE = 64
CHUNK = 128
SG_HEADS = 8
SG_HEAD_DIM = D_MIX // SG_HEADS

SUBLANES = 8
LANES = 128
MXU_DIM = 256

S5_SUPER = MXU_DIM // S5_GROUP
S5_NSUPER = S5_GROUPS // S5_SUPER
S5_SUPER_STATE = S5_SUPER * S5_STATE
S5_NSTATE = S5_GROUPS * S5_STATE

ROW_TILE = 512
S5_STEPS = ROW_TILE // BATCH
S5_PERM_STEPS = MXU_DIM // BATCH
FF_TILE = 1024
SCAN_COLS = 512
SCAN_UNROLL = 8
VMEM_LIMIT = 60 * 1024 * 1024

SHIFT1, SCALE1, GATE1, SHIFT2, SCALE2, GATE2 = range(6)

F32 = jnp.float32
BF16 = jnp.bfloat16


def _dot(a, b):
    return jnp.dot(a, b, preferred_element_type=F32)


def _rmsnorm(x, g):
    y = x * lax.rsqrt(jnp.mean(x * x, axis=-1, keepdims=True) + EPS)
    return y * g


def _const_spec(shape):
    zeros = (0,) * len(shape)
    return pl.BlockSpec(shape, lambda *_: zeros)


def _layer_spec(shape, layer):
    zeros = (0,) * len(shape)
    return pl.BlockSpec((None,) + tuple(shape), lambda *_: (layer,) + zeros)


def _params(*semantics):
    return pltpu.CompilerParams(dimension_semantics=semantics, vmem_limit_bytes=VMEM_LIMIT)


def _adaln_kernel(c_ref, w_ref, b_ref, o_ref):
    c = c_ref[...]
    c_act = (c * jax.nn.sigmoid(c)).astype(BF16)
    o_ref[...] = _dot(c_act, w_ref[...].astype(BF16)) + b_ref[...]


def _adaln(c, ada_w, ada_b):
    d = D_MODEL
    return pl.pallas_call(
        _adaln_kernel,
        grid=(DEPTH, 6),
        in_specs=[
            _const_spec((BATCH, d)),
            pl.BlockSpec((None, d, d), lambda i, j: (i, 0, j)),
            pl.BlockSpec((None, 1, d), lambda i, j: (i, 0, j)),
        ],
        out_specs=pl.BlockSpec((None, None, BATCH, d), lambda i, j: (i, j, 0, 0)),
        out_shape=jax.ShapeDtypeStruct((DEPTH, 6, BATCH, d), F32),
        compiler_params=_params("arbitrary", "arbitrary"),
        name="adaln",
    )(c, ada_w, ada_b.reshape(DEPTH, 1, 6 * d))


def _row_spec():
    return pl.BlockSpec((None, ROW_TILE, D_MODEL), lambda b, l: (b, l, 0))


def _mod_spec(layer):
    return _layer_spec((6, BATCH, D_MODEL), layer)


def _mod_row(mod_ref, which):
    return mod_ref[which, pl.ds(pl.program_id(0), 1), :]


def _mlp_kernel(x_ref, mod_ref, g_ref, w1_ref, w2_ref, fg_ref, o_ref, *, layer, final):
    x = x_ref[...]
    h = _rmsnorm(x, g_ref[layer:layer + 1, :]) * (1.0 + _mod_row(mod_ref, SCALE2))
    h = (h + _mod_row(mod_ref, SHIFT2)).astype(BF16)
    acc = jnp.zeros((ROW_TILE, D_MODEL), F32)
    for k in range(D_FF // FF_TILE):
        hid = _dot(h, w1_ref[:, k * FF_TILE:(k + 1) * FF_TILE])
        hid = jnp.square(jnp.maximum(hid, 0.0)).astype(BF16)
        acc = acc + _dot(hid, w2_ref[k * FF_TILE:(k + 1) * FF_TILE, :])
    out = x + _mod_row(mod_ref, GATE2) * acc
    if final:
        out = _rmsnorm(out, fg_ref[...])
    o_ref[...] = out


def _mlp_layer(x, mods, norm_g, w1, w2, final_g, *, layer):
    d = D_MODEL
    return pl.pallas_call(
        functools.partial(_mlp_kernel, layer=layer, final=layer == DEPTH - 1),
        grid=(BATCH, SEQ // ROW_TILE),
        in_specs=[_row_spec(), _mod_spec(layer), _const_spec((DEPTH, d)),
                  _layer_spec((d, D_FF), layer), _layer_spec((D_FF, d), layer),
                  _const_spec((1, d))],
        out_specs=_row_spec(),
        out_shape=jax.ShapeDtypeStruct((BATCH, SEQ, d), F32),
        compiler_params=_params("parallel", "arbitrary"),
        name="mlp",
    )(x, mods, norm_g, w1, w2, final_g)


def _conv_kernel(x_ref, mod_ref, g_ref, win_ref, cw_ref, cb_ref, wout_ref, o_ref, zbuf, *,
                 layer, idx):
    d = D_MIX
    halo = SUBLANES

    @pl.when(pl.program_id(1) == 0)
    def _():
        zbuf[0:halo, :] = jnp.zeros((halo, d), F32)

    x = x_ref[...]
    h = _rmsnorm(x, g_ref[layer:layer + 1, :]) * (1.0 + _mod_row(mod_ref, SCALE1))
    h = (h + _mod_row(mod_ref, SHIFT1)).astype(BF16)
    bcx = _dot(h, win_ref[...])
    b_gate = bcx[:, 0:d]
    z = bcx[:, d:2 * d] * bcx[:, 2 * d:3 * d]
    zbuf[halo:halo + ROW_TILE, :] = z
    conv = (cw_ref[idx, 2:3, :] * z
            + cw_ref[idx, 1:2, :] * zbuf[halo - 1:halo - 1 + ROW_TILE, :]
            + cw_ref[idx, 0:1, :] * zbuf[halo - 2:halo - 2 + ROW_TILE, :]
            + cb_ref[idx:idx + 1, :])
    zbuf[0:halo, :] = zbuf[ROW_TILE:ROW_TILE + halo, :]
    y = _dot((b_gate * conv).astype(BF16), wout_ref[...])
    o_ref[...] = x + _mod_row(mod_ref, GATE1) * y


def _conv_layer(x, mods, norm_g, w_in, cw, cb, w_out, *, layer, idx):
    d = D_MODEL
    n = cw.shape[0]
    return pl.pallas_call(
        functools.partial(_conv_kernel, layer=layer, idx=idx),
        grid=(BATCH, SEQ // ROW_TILE),
        in_specs=[_row_spec(), _mod_spec(layer), _const_spec((DEPTH, d)),
                  _layer_spec((d, 3 * D_MIX), idx), _const_spec((n, CONV_WIDTH, D_MIX)),
                  _const_spec((n, D_MIX)), _layer_spec((D_MIX, d), idx)],
        out_specs=_row_spec(),
        out_shape=jax.ShapeDtypeStruct((BATCH, SEQ, d), F32),
        scratch_shapes=[pltpu.VMEM((ROW_TILE + SUBLANES, D_MIX), F32)],
        compiler_params=_params("parallel", "arbitrary"),
        name="conv_mixer",
    )(x, mods, norm_g, w_in, cw, cb, w_out)


def _sg_kernel(x_ref, mod_ref, g_ref, win_ref, vg_ref, ws_ref, bs_ref, wout_ref, o_ref, vm_ref, *,
               layer, idx):
    d = D_MIX
    x = x_ref[...]
    h = _rmsnorm(x, g_ref[layer:layer + 1, :]) * (1.0 + _mod_row(mod_ref, SCALE1))
    h = (h + _mod_row(mod_ref, SHIFT1)).astype(BF16)
    uv = _dot(h, win_ref[...])
    u = uv[:, 0:d]
    v = _rmsnorm(uv[:, d:2 * d], vg_ref[idx:idx + 1, :]).astype(BF16)
    row = lax.broadcasted_iota(jnp.int32, (CHUNK, CHUNK), 0)
    col = lax.broadcasted_iota(jnp.int32, (CHUNK, CHUNK), 1)
    causal = col <= row
    for hd in range(SG_HEADS):
        ws = jnp.where(causal, ws_ref[hd], 0.0).astype(BF16)
        bias = bs_ref[:, hd:hd + 1]
        lanes = slice(hd * SG_HEAD_DIM, (hd + 1) * SG_HEAD_DIM)
        for n in range(ROW_TILE // CHUNK):
            rows = slice(n * CHUNK, (n + 1) * CHUNK)
            vm_ref[rows, lanes] = _dot(ws, v[rows, lanes]) + bias
    y = _dot((u * vm_ref[...]).astype(BF16), wout_ref[...])
    o_ref[...] = x + _mod_row(mod_ref, GATE1) * y


def _sg_layer(x, mods, norm_g, w_in, v_g, w_s, b_s_t, w_out, *, layer, idx):
    d = D_MODEL
    n = v_g.shape[0]
    return pl.pallas_call(
        functools.partial(_sg_kernel, layer=layer, idx=idx),
        grid=(BATCH, SEQ // ROW_TILE),
        in_specs=[_row_spec(), _mod_spec(layer), _const_spec((DEPTH, d)),
                  _layer_spec((d, 2 * D_MIX), idx), _const_spec((n, D_MIX)),
                  _layer_spec((SG_HEADS, CHUNK, CHUNK), idx), _layer_spec((CHUNK, SG_HEADS), idx),
                  _layer_spec((D_MIX, d), idx)],
        out_specs=_row_spec(),
        out_shape=jax.ShapeDtypeStruct((BATCH, SEQ, d), F32),
        scratch_shapes=[pltpu.VMEM((ROW_TILE, D_MIX), F32)],
        compiler_params=_params("parallel", "arbitrary"),
        name="sg_mixer",
    )(x, mods, norm_g, w_in, v_g, w_s, b_s_t, w_out)


def _s5_prep_kernel(are_ref, aim_ref, ldt_ref, bre_ref, bim_ref,
                    abar_re_ref, abar_im_ref, bbar_re_ref, bbar_im_ref):
    a_re = are_ref[...]
    a_im = aim_ref[...]
    dt = jnp.exp(ldt_ref[...])
    mag = jnp.exp(a_re * dt)
    abar_re = mag * jnp.cos(a_im * dt)
    abar_im = mag * jnp.sin(a_im * dt)
    den = a_re * a_re + a_im * a_im
    nr = abar_re - 1.0
    ni = abar_im
    f_re = ((nr * a_re + ni * a_im) / den)[:, None, :]
    f_im = ((ni * a_re - nr * a_im) / den)[:, None, :]
    b_re = bre_ref[...]
    b_im = bim_ref[...]
    abar_re_ref[...] = abar_re
    abar_im_ref[...] = abar_im
    bbar_re_ref[...] = f_re * b_re - f_im * b_im
    bbar_im_ref[...] = f_re * b_im + f_im * b_re


def _s5_prep(a_re, a_im, log_dt, b_re, b_im):
    g, p, h = S5_GROUPS, S5_STATE, S5_GROUP
    return pl.pallas_call(
        _s5_prep_kernel,
        out_shape=[jax.ShapeDtypeStruct((g, p), F32), jax.ShapeDtypeStruct((g, p), F32),
                   jax.ShapeDtypeStruct((g, h, p), F32), jax.ShapeDtypeStruct((g, h, p), F32)],
        name="s5_prep",
    )(a_re, a_im, log_dt.reshape(g, 1), b_re.transpose(0, 2, 1), b_im.transpose(0, 2, 1))


def _block_diag_in(w):
    w = w.reshape(S5_NSUPER, S5_SUPER, S5_GROUP, S5_STATE)
    eye = jnp.eye(S5_SUPER, dtype=w.dtype)
    out = w[:, :, :, None, :] * eye[None, :, None, :, None]
    return out.reshape(S5_NSUPER, MXU_DIM, S5_SUPER_STATE)


def _block_diag_out(w):
    return _block_diag_in(w).transpose(0, 2, 1)


def _time_major_perm():
    r = lax.broadcasted_iota(jnp.int32, (MXU_DIM, MXU_DIM), 0)
    c = lax.broadcasted_iota(jnp.int32, (MXU_DIM, MXU_DIM), 1)
    return r, c, (r % BATCH) * S5_PERM_STEPS + r // BATCH


def _s5_kernel(x_ref, mod_ref, g_ref, win_ref, are_ref, aim_ref, wbre_ref, wbim_ref,
               wcre_ref, wcim_ref, dskip_ref, gluw_ref, glub_ref, wout_ref, o_ref,
               h_ref, bre_ref, bim_ref, sre_ref, sim_ref, state_re, state_im, y_ref, *,
               layer, idx):
    d = D_MODEL

    @pl.when(pl.program_id(0) == 0)
    def _():
        state_re[...] = jnp.zeros_like(state_re)
        state_im[...] = jnp.zeros_like(state_im)

    r, c, src = _time_major_perm()
    to_time_major = jnp.where(c == src, 1.0, 0.0).astype(BF16)
    to_batch_major = jnp.where(r == (c % BATCH) * S5_PERM_STEPS + c // BATCH, 1.0, 0.0).astype(BF16)

    g = g_ref[layer:layer + 1, :]
    for part in range(S5_STEPS // S5_PERM_STEPS):
        steps = slice(part * S5_PERM_STEPS, (part + 1) * S5_PERM_STEPS)
        h_rows = []
        for b in range(BATCH):
            h_b = _rmsnorm(x_ref[b, steps, :], g) * (1.0 + mod_ref[SCALE1, b:b + 1, :])
            h_rows.append((h_b + mod_ref[SHIFT1, b:b + 1, :]).astype(BF16))
        h_part = jnp.concatenate(h_rows, axis=0)
        h_ref[part * MXU_DIM:(part + 1) * MXU_DIM, :] = _dot(to_time_major, h_part).astype(BF16)

    u = _dot(h_ref[...], win_ref[...])
    ub = u.astype(BF16)
    pair = 2 * BATCH
    for k in range(S5_NSUPER):
        buf = k % 2
        uk = ub[:, k * MXU_DIM:(k + 1) * MXU_DIM]
        bre_ref[buf] = _dot(uk, wbre_ref[k])
        bim_ref[buf] = _dot(uk, wbim_ref[k])
        blocks = []
        for c0 in range(0, S5_SUPER_STATE, SCAN_COLS):
            gcols = slice(k * S5_SUPER_STATE + c0, k * S5_SUPER_STATE + c0 + SCAN_COLS)
            blocks.append(dict(
                cols=slice(c0, c0 + SCAN_COLS), gcols=gcols,
                a_re=jnp.broadcast_to(are_ref[:, gcols], (BATCH, SCAN_COLS)),
                a_im=jnp.broadcast_to(aim_ref[:, gcols], (BATCH, SCAN_COLS)),
                s_re=state_re[:, gcols], s_im=state_im[:, gcols]))
        for t2 in range(S5_STEPS // 2):
            for blk in blocks:
                new_re, new_im = [], []
                for t in (2 * t2, 2 * t2 + 1):
                    rows = slice(t * BATCH, (t + 1) * BATCH)
                    s_re, s_im = blk["s_re"], blk["s_im"]
                    blk["s_re"] = (blk["a_re"] * s_re - blk["a_im"] * s_im
                                   + bre_ref[buf, rows, blk["cols"]])
                    blk["s_im"] = (blk["a_re"] * s_im + blk["a_im"] * s_re
                                   + bim_ref[buf, rows, blk["cols"]])
                    new_re.append(blk["s_re"])
                    new_im.append(blk["s_im"])
                rows2 = slice(t2 * pair, (t2 + 1) * pair)
                sre_ref[buf, rows2, blk["cols"]] = jnp.concatenate(new_re, axis=0).astype(BF16)
                sim_ref[buf, rows2, blk["cols"]] = jnp.concatenate(new_im, axis=0).astype(BF16)
        for blk in blocks:
            state_re[:, blk["gcols"]] = blk["s_re"]
            state_im[:, blk["gcols"]] = blk["s_im"]
        y_ref[:, k * MXU_DIM:(k + 1) * MXU_DIM] = (
            _dot(sre_ref[buf], wcre_ref[k]) - _dot(sim_ref[buf], wcim_ref[k]))

    y = jax.nn.gelu(y_ref[...] + dskip_ref[idx:idx + 1, :] * u)
    y = y * jax.nn.sigmoid(_dot(y.astype(BF16), gluw_ref[...]) + glub_ref[idx:idx + 1, :])
    yb = y.astype(BF16)

    for part in range(S5_STEPS // S5_PERM_STEPS):
        steps = slice(part * S5_PERM_STEPS, (part + 1) * S5_PERM_STEPS)
        y_part = _dot(to_batch_major, yb[part * MXU_DIM:(part + 1) * MXU_DIM, :]).astype(BF16)
        out = _dot(y_part, wout_ref[...])
        for b in range(BATCH):
            out_b = out[b * S5_PERM_STEPS:(b + 1) * S5_PERM_STEPS, :]
            o_ref[b, steps, :] = x_ref[b, steps, :] + mod_ref[GATE1, b:b + 1, :] * out_b


def _s5_layer(x, mods, norm_g, w_in, abar_re, abar_im, wb_re, wb_im, wc_re, wc_im, d_skip,
              glu_w, glu_b, w_out, *, layer, idx):
    d = D_MODEL
    n = d_skip.shape[0]
    x_spec = pl.BlockSpec((BATCH, S5_STEPS, d), lambda i: (0, i, 0))
    return pl.pallas_call(
        functools.partial(_s5_kernel, layer=layer, idx=idx),
        grid=(SEQ // S5_STEPS,),
        in_specs=[x_spec, _mod_spec(layer), _const_spec((DEPTH, d)),
                  _layer_spec((d, D_MIX), idx),
                  _const_spec((1, S5_NSTATE)), _const_spec((1, S5_NSTATE)),
                  _const_spec((S5_NSUPER, MXU_DIM, S5_SUPER_STATE)),
                  _const_spec((S5_NSUPER, MXU_DIM, S5_SUPER_STATE)),
                  _const_spec((S5_NSUPER, S5_SUPER_STATE, MXU_DIM)),
                  _const_spec((S5_NSUPER, S5_SUPER_STATE, MXU_DIM)),
                  _const_spec((n, D_MIX)), _layer_spec((D_MIX, D_MIX), idx),
                  _const_spec((n, D_MIX)), _layer_spec((D_MIX, d), idx)],
        out_specs=x_spec,
        out_shape=jax.ShapeDtypeStruct((BATCH, SEQ, d), F32),
        scratch_shapes=[pltpu.VMEM((ROW_TILE, d), BF16),
                        pltpu.VMEM((2, ROW_TILE, S5_SUPER_STATE), F32),
                        pltpu.VMEM((2, ROW_TILE, S5_SUPER_STATE), F32),
                        pltpu.VMEM((2, ROW_TILE, S5_SUPER_STATE), BF16),
                        pltpu.VMEM((2, ROW_TILE, S5_SUPER_STATE), BF16),
                        pltpu.VMEM((BATCH, S5_NSTATE), F32),
                        pltpu.VMEM((BATCH, S5_NSTATE), F32),
                        pltpu.VMEM((ROW_TILE, D_MIX), F32)],
        compiler_params=_params("arbitrary"),
        name="s5_mixer",
    )(x, mods, norm_g, w_in, abar_re, abar_im, wb_re, wb_im, wc_re, wc_im, d_skip,
      glu_w, glu_b, w_out)


def kernel(x, c, ada_w, ada_b, norm1_g, norm2_g, ff_w1, ff_w2, final_g, conv_w_in, conv_w, conv_b, conv_w_out, ssm_w_in, ssm_a_re, ssm_a_im, ssm_log_dt, ssm_b_re, ssm_b_im, ssm_c_re, ssm_c_im, ssm_d, ssm_glu_w, ssm_glu_b, ssm_w_out, sg_w_in, sg_v_g, sg_w_s, sg_b_s, sg_w_out):
    mods = _adaln(c, ada_w, ada_b)
    bf = lambda w: w.astype(BF16)
    ff_w1, ff_w2 = bf(ff_w1), bf(ff_w2)
    conv_w_in, conv_w_out = bf(conv_w_in), bf(conv_w_out)
    ssm_w_in, ssm_glu_w, ssm_w_out = bf(ssm_w_in), bf(ssm_glu_w), bf(ssm_w_out)
    sg_w_in, sg_w_out = bf(sg_w_in), bf(sg_w_out)
    sg_b_s_t = sg_b_s.transpose(0, 2, 1)
    final_g = final_g.reshape(1, D_MODEL)
    for i in range(DEPTH):
        kind = i % N_MIXERS
        j = i // N_MIXERS
        if kind == 0:
            x = _conv_layer(x, mods, norm1_g, conv_w_in, conv_w, conv_b, conv_w_out,
                            layer=i, idx=j)
        elif kind == 1:
            abar_re, abar_im, bbar_re, bbar_im = _s5_prep(
                ssm_a_re[j], ssm_a_im[j], ssm_log_dt[j], ssm_b_re[j], ssm_b_im[j])
            x = _s5_layer(x, mods, norm1_g, ssm_w_in,
                          abar_re.reshape(1, S5_NSTATE), abar_im.reshape(1, S5_NSTATE),
                          bf(_block_diag_in(bbar_re)), bf(_block_diag_in(bbar_im)),
                          bf(_block_diag_out(ssm_c_re[j])), bf(_block_diag_out(ssm_c_im[j])),
                          ssm_d, ssm_glu_w, ssm_glu_b, ssm_w_out, layer=i, idx=j)
        else:
            x = _sg_layer(x, mods, norm1_g, sg_w_in, sg_v_g, sg_w_s, sg_b_s_t, sg_w_out,
                          layer=i, idx=j)
        x = _mlp_layer(x, mods, norm2_g, ff_w1, ff_w2, final_g, layer=i)
    return x
```

```python
import functools

import jax
import jax.numpy as jnp
from jax import lax
from jax.experimental import pallas as pl
from jax.experimental.pallas import tpu as pltpu

D_MODEL = 1024
BATCH = 8
SEQ = 4096
DEPTH = 4
N_MIXERS = 3
D_FF = 4 * D_MODEL
D_MIX = D_MODEL
EPS = 1e-6
CONV_WIDTH = 3
S5_GROUP = 16
S5_GROUPS = D_MIX // S5_GROUP
S5_STATE = 64
CHUNK = 128
SG_HEADS = 8
SG_HEAD_DIM = D_MIX // SG_HEADS

SUBLANES = 8
LANES = 128
MXU_DIM = 256

S5_SUPER = MXU_DIM // S5_GROUP
S5_NSUPER = S5_GROUPS // S5_SUPER
S5_SUPER_STATE = S5_SUPER * S5_STATE
S5_NSTATE = S5_GROUPS * S5_STATE

ROW_TILE = 1024
S5_ROWS = 512
S5_STEPS = S5_ROWS // BATCH
S5_PERM_STEPS = MXU_DIM // BATCH
FF_TILE = 1024
SCAN_COLS = 512
SCAN_UNROLL = 8
VMEM_LIMIT = 60 * 1024 * 1024

SHIFT1, SCALE1, GATE1, SHIFT2, SCALE2, GATE2 = range(6)

F32 = jnp.float32
BF16 = jnp.bfloat16


def _dot(a, b):
    return jnp.dot(a, b, preferred_element_type=F32)


def _rmsnorm(x, g):
    y = x * lax.rsqrt(jnp.mean(x * x, axis=-1, keepdims=True) + EPS)
    return y * g


def _const_spec(shape):
    zeros = (0,) * len(shape)
    return pl.BlockSpec(shape, lambda *_: zeros, pipeline_mode=pl.Buffered(1))


def _layer_spec(shape, layer):
    zeros = (0,) * len(shape)
    return pl.BlockSpec((None,) + tuple(shape), lambda *_: (layer,) + zeros,
                        pipeline_mode=pl.Buffered(1))


def _params(*semantics):
    return pltpu.CompilerParams(dimension_semantics=semantics, vmem_limit_bytes=VMEM_LIMIT)


def _adaln_kernel(c_ref, w_ref, b_ref, o_ref):
    c = c_ref[...]
    c_act = (c * jax.nn.sigmoid(c)).astype(BF16)
    o_ref[...] = _dot(c_act, w_ref[...].astype(BF16)) + b_ref[...]


def _adaln(c, ada_w, ada_b):
    d = D_MODEL
    return pl.pallas_call(
        _adaln_kernel,
        grid=(DEPTH, 6),
        in_specs=[
            _const_spec((BATCH, d)),
            pl.BlockSpec((None, d, d), lambda i, j: (i, 0, j)),
            pl.BlockSpec((None, 1, d), lambda i, j: (i, 0, j)),
        ],
        out_specs=pl.BlockSpec((None, None, BATCH, d), lambda i, j: (i, j, 0, 0)),
        out_shape=jax.ShapeDtypeStruct((DEPTH, 6, BATCH, d), F32),
        compiler_params=_params("arbitrary", "arbitrary"),
        name="adaln",
    )(c, ada_w, ada_b.reshape(DEPTH, 1, 6 * d))


def _row_spec():
    return pl.BlockSpec((None, ROW_TILE, D_MODEL), lambda b, l: (b, l, 0))


def _mod_spec(layer):
    return _layer_spec((6, BATCH, D_MODEL), layer)


def _mod_row(mod_ref, which):
    return mod_ref[which, pl.ds(pl.program_id(0), 1), :]


def _mlp_kernel(x_ref, mod_ref, g_ref, w1_ref, w2_ref, fg_ref, o_ref, *, layer, final):
    x = x_ref[...]
    h = _rmsnorm(x, g_ref[layer:layer + 1, :]) * (1.0 + _mod_row(mod_ref, SCALE2))
    h = (h + _mod_row(mod_ref, SHIFT2)).astype(BF16)
    acc = jnp.zeros((ROW_TILE, D_MODEL), F32)
    for k in range(D_FF // FF_TILE):
        hid = _dot(h, w1_ref[:, k * FF_TILE:(k + 1) * FF_TILE])
        hid = jnp.square(jnp.maximum(hid, 0.0)).astype(BF16)
        acc = acc + _dot(hid, w2_ref[k * FF_TILE:(k + 1) * FF_TILE, :])
    out = x + _mod_row(mod_ref, GATE2) * acc
    if final:
        out = _rmsnorm(out, fg_ref[...])
    o_ref[...] = out


def _mlp_layer(x, mods, norm_g, w1, w2, final_g, *, layer):
    d = D_MODEL
    return pl.pallas_call(
        functools.partial(_mlp_kernel, layer=layer, final=layer == DEPTH - 1),
        grid=(BATCH, SEQ // ROW_TILE),
        in_specs=[_row_spec(), _mod_spec(layer), _const_spec((DEPTH, d)),
                  _layer_spec((d, D_FF), layer), _layer_spec((D_FF, d), layer),
                  _const_spec((1, d))],
        out_specs=_row_spec(),
        out_shape=jax.ShapeDtypeStruct((BATCH, SEQ, d), F32),
        compiler_params=_params("parallel", "arbitrary"),
        name="mlp",
    )(x, mods, norm_g, w1, w2, final_g)


def _conv_kernel(x_ref, mod_ref, g_ref, win_ref, cw_ref, cb_ref, wout_ref, o_ref, zbuf, *,
                 layer, idx):
    d = D_MIX
    halo = SUBLANES

    @pl.when(pl.program_id(1) == 0)
    def _():
        zbuf[0:halo, :] = jnp.zeros((halo, d), F32)

    x = x_ref[...]
    h = _rmsnorm(x, g_ref[layer:layer + 1, :]) * (1.0 + _mod_row(mod_ref, SCALE1))
    h = (h + _mod_row(mod_ref, SHIFT1)).astype(BF16)
    bcx = _dot(h, win_ref[...])
    b_gate = bcx[:, 0:d]
    z = bcx[:, d:2 * d] * bcx[:, 2 * d:3 * d]
    zbuf[halo:halo + ROW_TILE, :] = z
    conv = (cw_ref[idx, 2:3, :] * z
            + cw_ref[idx, 1:2, :] * zbuf[halo - 1:halo - 1 + ROW_TILE, :]
            + cw_ref[idx, 0:1, :] * zbuf[halo - 2:halo - 2 + ROW_TILE, :]
            + cb_ref[idx:idx + 1, :])
    zbuf[0:halo, :] = zbuf[ROW_TILE:ROW_TILE + halo, :]
    y = _dot((b_gate * conv).astype(BF16), wout_ref[...])
    o_ref[...] = x + _mod_row(mod_ref, GATE1) * y


def _conv_layer(x, mods, norm_g, w_in, cw, cb, w_out, *, layer, idx):
    d = D_MODEL
    n = cw.shape[0]
    return pl.pallas_call(
        functools.partial(_conv_kernel, layer=layer, idx=idx),
        grid=(BATCH, SEQ // ROW_TILE),
        in_specs=[_row_spec(), _mod_spec(layer), _const_spec((DEPTH, d)),
                  _layer_spec((d, 3 * D_MIX), idx), _const_spec((n, CONV_WIDTH, D_MIX)),
                  _const_spec((n, D_MIX)), _layer_spec((D_MIX, d), idx)],
        out_specs=_row_spec(),
        out_shape=jax.ShapeDtypeStruct((BATCH, SEQ, d), F32),
        scratch_shapes=[pltpu.VMEM((ROW_TILE + SUBLANES, D_MIX), F32)],
        compiler_params=_params("parallel", "arbitrary"),
        name="conv_mixer",
    )(x, mods, norm_g, w_in, cw, cb, w_out)


def _sg_kernel(x_ref, mod_ref, g_ref, win_ref, vg_ref, ws_ref, bs_ref, wout_ref, o_ref, vm_ref, *,
               layer, idx):
    d = D_MIX
    x = x_ref[...]
    h = _rmsnorm(x, g_ref[layer:layer + 1, :]) * (1.0 + _mod_row(mod_ref, SCALE1))
    h = (h + _mod_row(mod_ref, SHIFT1)).astype(BF16)
    uv = _dot(h, win_ref[...])
    u = uv[:, 0:d]
    v = _rmsnorm(uv[:, d:2 * d], vg_ref[idx:idx + 1, :]).astype(BF16)
    row = lax.broadcasted_iota(jnp.int32, (CHUNK, CHUNK), 0)
    col = lax.broadcasted_iota(jnp.int32, (CHUNK, CHUNK), 1)
    causal = col <= row
    for hd in range(SG_HEADS):
        ws = jnp.where(causal, ws_ref[hd], 0.0).astype(BF16)
        bias = bs_ref[:, hd:hd + 1]
        lanes = slice(hd * SG_HEAD_DIM, (hd + 1) * SG_HEAD_DIM)
        for n in range(ROW_TILE // CHUNK):
            rows = slice(n * CHUNK, (n + 1) * CHUNK)
            vm_ref[rows, lanes] = _dot(ws, v[rows, lanes]) + bias
    y = _dot((u * vm_ref[...]).astype(BF16), wout_ref[...])
    o_ref[...] = x + _mod_row(mod_ref, GATE1) * y


def _sg_layer(x, mods, norm_g, w_in, v_g, w_s, b_s_t, w_out, *, layer, idx):
    d = D_MODEL
    n = v_g.shape[0]
    return pl.pallas_call(
        functools.partial(_sg_kernel, layer=layer, idx=idx),
        grid=(BATCH, SEQ // ROW_TILE),
        in_specs=[_row_spec(), _mod_spec(layer), _const_spec((DEPTH, d)),
                  _layer_spec((d, 2 * D_MIX), idx), _const_spec((n, D_MIX)),
                  _layer_spec((SG_HEADS, CHUNK, CHUNK), idx), _layer_spec((CHUNK, SG_HEADS), idx),
                  _layer_spec((D_MIX, d), idx)],
        out_specs=_row_spec(),
        out_shape=jax.ShapeDtypeStruct((BATCH, SEQ, d), F32),
        scratch_shapes=[pltpu.VMEM((ROW_TILE, D_MIX), F32)],
        compiler_params=_params("parallel", "arbitrary"),
        name="sg_mixer",
    )(x, mods, norm_g, w_in, v_g, w_s, b_s_t, w_out)


def _replicate_block_diag(w, n_rep, row_block, col_block):
    rows = w.shape[0]
    src = lax.broadcasted_iota(jnp.int32, (col_block, n_rep * col_block), 0)
    dst = lax.broadcasted_iota(jnp.int32, (col_block, n_rep * col_block), 1)
    replicate = jnp.where(dst % col_block == src, 1.0, 0.0).astype(BF16)
    tiled = _dot(w.astype(BF16), replicate)
    r = lax.broadcasted_iota(jnp.int32, (rows, n_rep * col_block), 0)
    c = lax.broadcasted_iota(jnp.int32, (rows, n_rep * col_block), 1)
    return jnp.where(r // row_block == c // col_block, tiled, 0.0).astype(BF16)


def _s5_prep_kernel(are_ref, aim_ref, ldt_ref, bre_ref, bim_ref, cre_ref, cim_ref,
                    abar_re_ref, abar_im_ref, wbre_ref, wbim_ref, wcre_ref, wcim_ref):
    a_re = are_ref[...]
    a_im = aim_ref[...]
    dt = jnp.exp(ldt_ref[...])
    mag = jnp.exp(a_re * dt)
    abar_re = mag * jnp.cos(a_im * dt)
    abar_im = mag * jnp.sin(a_im * dt)
    den = a_re * a_re + a_im * a_im
    nr = abar_re - 1.0
    ni = abar_im
    f_re = ((nr * a_re + ni * a_im) / den)[:, None, :]
    f_im = ((ni * a_re - nr * a_im) / den)[:, None, :]
    b_re = bre_ref[...]
    b_im = bim_ref[...]
    abar_re_ref[...] = abar_re
    abar_im_ref[...] = abar_im
    bbar_re = (f_re * b_re - f_im * b_im).reshape(S5_GROUPS * S5_GROUP, S5_STATE)
    bbar_im = (f_re * b_im + f_im * b_re).reshape(S5_GROUPS * S5_GROUP, S5_STATE)
    for k in range(S5_NSUPER):
        rows_in = slice(k * MXU_DIM, (k + 1) * MXU_DIM)
        wbre_ref[k] = _replicate_block_diag(bbar_re[rows_in, :], S5_SUPER, S5_GROUP, S5_STATE)
        wbim_ref[k] = _replicate_block_diag(bbar_im[rows_in, :], S5_SUPER, S5_GROUP, S5_STATE)
        rows_out = slice(k * S5_SUPER_STATE, (k + 1) * S5_SUPER_STATE)
        wcre_ref[k] = _replicate_block_diag(cre_ref[rows_out, :], S5_SUPER, S5_STATE, S5_GROUP)
        wcim_ref[k] = _replicate_block_diag(cim_ref[rows_out, :], S5_SUPER, S5_STATE, S5_GROUP)


def _s5_prep(a_re, a_im, log_dt, b_re, b_im, c_re, c_im):
    g, p, h = S5_GROUPS, S5_STATE, S5_GROUP
    state_major = lambda c: c.transpose(0, 2, 1).reshape(g * p, h)
    return pl.pallas_call(
        _s5_prep_kernel,
        out_shape=[jax.ShapeDtypeStruct((g, p), F32), jax.ShapeDtypeStruct((g, p), F32),
                   jax.ShapeDtypeStruct((S5_NSUPER, MXU_DIM, S5_SUPER_STATE), BF16),
                   jax.ShapeDtypeStruct((S5_NSUPER, MXU_DIM, S5_SUPER_STATE), BF16),
                   jax.ShapeDtypeStruct((S5_NSUPER, S5_SUPER_STATE, MXU_DIM), BF16),
                   jax.ShapeDtypeStruct((S5_NSUPER, S5_SUPER_STATE, MXU_DIM), BF16)],
        name="s5_prep",
    )(a_re, a_im, log_dt.reshape(g, 1), b_re.transpose(0, 2, 1), b_im.transpose(0, 2, 1),
      state_major(c_re), state_major(c_im))


def _time_major_perm():
    r = lax.broadcasted_iota(jnp.int32, (MXU_DIM, MXU_DIM), 0)
    c = lax.broadcasted_iota(jnp.int32, (MXU_DIM, MXU_DIM), 1)
    return r, c, (r % BATCH) * S5_PERM_STEPS + r // BATCH


def _s5_kernel(x_ref, mod_ref, g_ref, win_ref, are_ref, aim_ref, wbre_ref, wbim_ref,
               wcre_ref, wcim_ref, dskip_ref, gluw_ref, glub_ref, wout_ref, o_ref,
               h_ref, bre_ref, bim_ref, sre_ref, sim_ref, state_re, state_im, y_ref, *,
               layer, idx):
    d = D_MODEL

    @pl.when(pl.program_id(0) == 0)
    def _():
        state_re[...] = jnp.zeros_like(state_re)
        state_im[...] = jnp.zeros_like(state_im)

    r, c, src = _time_major_perm()
    to_time_major = jnp.where(c == src, 1.0, 0.0).astype(BF16)
    to_batch_major = jnp.where(r == (c % BATCH) * S5_PERM_STEPS + c // BATCH, 1.0, 0.0).astype(BF16)

    g = g_ref[layer:layer + 1, :]
    for part in range(S5_STEPS // S5_PERM_STEPS):
        steps = slice(part * S5_PERM_STEPS, (part + 1) * S5_PERM_STEPS)
        h_rows = []
        for b in range(BATCH):
            h_b = _rmsnorm(x_ref[b, steps, :], g) * (1.0 + mod_ref[SCALE1, b:b + 1, :])
            h_rows.append((h_b + mod_ref[SHIFT1, b:b + 1, :]).astype(BF16))
        h_part = jnp.concatenate(h_rows, axis=0)
        h_ref[part * MXU_DIM:(part + 1) * MXU_DIM, :] = _dot(to_time_major, h_part).astype(BF16)

    u = _dot(h_ref[...], win_ref[...])
    ub = u.astype(BF16)
    pair = 2 * BATCH
    for k in range(S5_NSUPER):
        buf = k % 2
        uk = ub[:, k * MXU_DIM:(k + 1) * MXU_DIM]
        bre_ref[buf] = _dot(uk, wbre_ref[k])
        bim_ref[buf] = _dot(uk, wbim_ref[k])
        blocks = []
        for c0 in range(0, S5_SUPER_STATE, SCAN_COLS):
            gcols = slice(k * S5_SUPER_STATE + c0, k * S5_SUPER_STATE + c0 + SCAN_COLS)
            blocks.append(dict(
                cols=slice(c0, c0 + SCAN_COLS), gcols=gcols,
                a_re=jnp.broadcast_to(are_ref[:, gcols], (BATCH, SCAN_COLS)),
                a_im=jnp.broadcast_to(aim_ref[:, gcols], (BATCH, SCAN_COLS)),
                s_re=state_re[:, gcols], s_im=state_im[:, gcols]))
        for t2 in range(S5_STEPS // 2):
            for blk in blocks:
                new_re, new_im = [], []
                for t in (2 * t2, 2 * t2 + 1):
                    rows = slice(t * BATCH, (t + 1) * BATCH)
                    s_re, s_im = blk["s_re"], blk["s_im"]
                    blk["s_re"] = (blk["a_re"] * s_re - blk["a_im"] * s_im
                                   + bre_ref[buf, rows, blk["cols"]])
                    blk["s_im"] = (blk["a_re"] * s_im + blk["a_im"] * s_re
                                   + bim_ref[buf, rows, blk["cols"]])
                    new_re.append(blk["s_re"])
                    new_im.append(blk["s_im"])
                rows2 = slice(t2 * pair, (t2 + 1) * pair)
                sre_ref[buf, rows2, blk["cols"]] = jnp.concatenate(new_re, axis=0).astype(BF16)
                sim_ref[buf, rows2, blk["cols"]] = jnp.concatenate(new_im, axis=0).astype(BF16)
        for blk in blocks:
            state_re[:, blk["gcols"]] = blk["s_re"]
            state_im[:, blk["gcols"]] = blk["s_im"]
        y_ref[:, k * MXU_DIM:(k + 1) * MXU_DIM] = (
            _dot(sre_ref[buf], wcre_ref[k]) - _dot(sim_ref[buf], wcim_ref[k]))

    y = jax.nn.gelu(y_ref[...] + dskip_ref[idx:idx + 1, :] * u)
    y = y * jax.nn.sigmoid(_dot(y.astype(BF16), gluw_ref[...]) + glub_ref[idx:idx + 1, :])
    yb = y.astype(BF16)

    for part in range(S5_STEPS // S5_PERM_STEPS):
        steps = slice(part * S5_PERM_STEPS, (part + 1) * S5_PERM_STEPS)
        y_part = _dot(to_batch_major, yb[part * MXU_DIM:(part + 1) * MXU_DIM, :]).astype(BF16)
        out = _dot(y_part, wout_ref[...])
        for b in range(BATCH):
            out_b = out[b * S5_PERM_STEPS:(b + 1) * S5_PERM_STEPS, :]
            o_ref[b, steps, :] = x_ref[b, steps, :] + mod_ref[GATE1, b:b + 1, :] * out_b


def _s5_layer(x, mods, norm_g, w_in, abar_re, abar_im, wb_re, wb_im, wc_re, wc_im, d_skip,
              glu_w, glu_b, w_out, *, layer, idx):
    d = D_MODEL
    n = d_skip.shape[0]
    x_spec = pl.BlockSpec((BATCH, S5_STEPS, d), lambda i: (0, i, 0))
    return pl.pallas_call(
        functools.partial(_s5_kernel, layer=layer, idx=idx),
        grid=(SEQ // S5_STEPS,),
        in_specs=[x_spec, _mod_spec(layer), _const_spec((DEPTH, d)),
                  _layer_spec((d, D_MIX), idx),
                  _const_spec((1, S5_NSTATE)), _const_spec((1, S5_NSTATE)),
                  _const_spec((S5_NSUPER, MXU_DIM, S5_SUPER_STATE)),
                  _const_spec((S5_NSUPER, MXU_DIM, S5_SUPER_STATE)),
                  _const_spec((S5_NSUPER, S5_SUPER_STATE, MXU_DIM)),
                  _const_spec((S5_NSUPER, S5_SUPER_STATE, MXU_DIM)),
                  _const_spec((n, D_MIX)), _layer_spec((D_MIX, D_MIX), idx),
                  _const_spec((n, D_MIX)), _layer_spec((D_MIX, d), idx)],
        out_specs=x_spec,
        out_shape=jax.ShapeDtypeStruct((BATCH, SEQ, d), F32),
        scratch_shapes=[pltpu.VMEM((S5_ROWS, d), BF16),
                        pltpu.VMEM((2, S5_ROWS, S5_SUPER_STATE), F32),
                        pltpu.VMEM((2, S5_ROWS, S5_SUPER_STATE), F32),
                        pltpu.VMEM((2, S5_ROWS, S5_SUPER_STATE), BF16),
                        pltpu.VMEM((2, S5_ROWS, S5_SUPER_STATE), BF16),
                        pltpu.VMEM((BATCH, S5_NSTATE), F32),
                        pltpu.VMEM((BATCH, S5_NSTATE), F32),
                        pltpu.VMEM((S5_ROWS, D_MIX), F32)],
        compiler_params=_params("arbitrary"),
        name="s5_mixer",
    )(x, mods, norm_g, w_in, abar_re, abar_im, wb_re, wb_im, wc_re, wc_im, d_skip,
      glu_w, glu_b, w_out)


def kernel(x, c, ada_w, ada_b, norm1_g, norm2_g, ff_w1, ff_w2, final_g, conv_w_in, conv_w, conv_b, conv_w_out, ssm_w_in, ssm_a_re, ssm_a_im, ssm_log_dt, ssm_b_re, ssm_b_im, ssm_c_re, ssm_c_im, ssm_d, ssm_glu_w, ssm_glu_b, ssm_w_out, sg_w_in, sg_v_g, sg_w_s, sg_b_s, sg_w_out):
    mods = _adaln(c, ada_w, ada_b)
    bf = lambda w: w.astype(BF16)
    ff_w1, ff_w2 = bf(ff_w1), bf(ff_w2)
    conv_w_in, conv_w_out = bf(conv_w_in), bf(conv_w_out)
    ssm_w_in, ssm_glu_w, ssm_w_out = bf(ssm_w_in), bf(ssm_glu_w), bf(ssm_w_out)
    sg_w_in, sg_w_out = bf(sg_w_in), bf(sg_w_out)
    sg_b_s_t = sg_b_s.transpose(0, 2, 1)
    final_g = final_g.reshape(1, D_MODEL)
    for i in range(DEPTH):
        kind = i % N_MIXERS
        j = i // N_MIXERS
        if kind == 0:
            x = _conv_layer(x, mods, norm1_g, conv_w_in, conv_w, conv_b, conv_w_out,
                            layer=i, idx=j)
        elif kind == 1:
            abar_re, abar_im, wb_re, wb_im, wc_re, wc_im = _s5_prep(
                ssm_a_re[j], ssm_a_im[j], ssm_log_dt[j], ssm_b_re[j], ssm_b_im[j],
                ssm_c_re[j], ssm_c_im[j])
            x = _s5_layer(x, mods, norm1_g, ssm_w_in,
                          abar_re.reshape(1, S5_NSTATE), abar_im.reshape(1, S5_NSTATE),
                          wb_re, wb_im, wc_re, wc_im,
                          ssm_d, ssm_glu_w, ssm_glu_b, ssm_w_out, layer=i, idx=j)
        else:
            x = _sg_layer(x, mods, norm1_g, sg_w_in, sg_v_g, sg_w_s, sg_b_s_t, sg_w_out,
                          layer=i, idx=j)
        x = _mlp_layer(x, mods, norm2_g, ff_w1, ff_w2, final_g, layer=i)
    return x
```

```python
import functools

import jax
import jax.numpy as jnp
from jax import lax
from jax.experimental import pallas as pl
from jax.experimental.pallas import tpu as pltpu

D_MODEL = 1024
BATCH = 8
SEQ = 4096
DEPTH = 4
N_MIXERS = 3
D_FF = 4 * D_MODEL
D_MIX = D_MODEL
EPS = 1e-6
CONV_WIDTH = 3
S5_GROUP = 16
S5_GROUPS = D_MIX // S5_GROUP
S5_STATE = 64
CHUNK = 128
SG_HEADS = 8
SG_HEAD_DIM = D_MIX // SG_HEADS

SUBLANES = 8
LANES = 128
MXU_DIM = 256

S5_SUPER = MXU_DIM // S5_GROUP
S5_NSUPER = S5_GROUPS // S5_SUPER
S5_SUPER_STATE = S5_SUPER * S5_STATE
S5_NSTATE = S5_GROUPS * S5_STATE

ROW_TILE = 1024
S5_ROWS = 512
S5_STEPS = S5_ROWS // BATCH
S5_PERM_STEPS = MXU_DIM // BATCH
FF_TILE = 1024
SCAN_COLS = 512
SCAN_UNROLL = 8
VMEM_LIMIT = 60 * 1024 * 1024

SHIFT1, SCALE1, GATE1, SHIFT2, SCALE2, GATE2 = range(6)

F32 = jnp.float32
BF16 = jnp.bfloat16


def _dot(a, b):
    return jnp.dot(a, b, preferred_element_type=F32)


def _rmsnorm(x, g):
    y = x * lax.rsqrt(jnp.mean(x * x, axis=-1, keepdims=True) + EPS)
    return y * g


def _const_spec(shape):
    zeros = (0,) * len(shape)
    return pl.BlockSpec(shape, lambda *_: zeros, pipeline_mode=pl.Buffered(1))


def _layer_spec(shape, layer):
    zeros = (0,) * len(shape)
    return pl.BlockSpec((None,) + tuple(shape), lambda *_: (layer,) + zeros,
                        pipeline_mode=pl.Buffered(1))


def _params(*semantics):
    return pltpu.CompilerParams(dimension_semantics=semantics, vmem_limit_bytes=VMEM_LIMIT)


BF16_SUBLANES = 16


def _cast_plan(casts, n_steps, step_index):
    in_specs, out_specs, out_shapes, args = [], [], [], []
    for stacked, entry in casts:
        _, rows, cols = stacked.shape
        block_rows, rem = divmod(rows, n_steps)
        assert rem == 0 and block_rows % BF16_SUBLANES == 0, (rows, n_steps)
        in_specs.append(pl.BlockSpec((None, block_rows, cols),
                                     lambda *g, entry=entry: (entry, step_index(*g), 0)))
        out_specs.append(pl.BlockSpec((block_rows, cols), lambda *g: (step_index(*g), 0)))
        out_shapes.append(jax.ShapeDtypeStruct((rows, cols), BF16))
        args.append(stacked)
    return in_specs, out_specs, out_shapes, args


def _split_cast_refs(rest, n_cast):
    cast_in = rest[:n_cast]
    o_ref = rest[n_cast]
    cast_out = rest[n_cast + 1:2 * n_cast + 1]
    return list(zip(cast_in, cast_out)), o_ref, rest[2 * n_cast + 1:]


def _cast_blocks(casts):
    for src, dst in casts:
        dst[...] = src[...].astype(BF16)


def _tile_step(b, l):
    return b * (SEQ // ROW_TILE) + l


def _adaln_kernel(c_ref, w_ref, b_ref, o_ref):
    c = c_ref[...]
    c_act = (c * jax.nn.sigmoid(c)).astype(BF16)
    o_ref[...] = _dot(c_act, w_ref[...].astype(BF16)) + b_ref[...]


def _adaln(c, ada_w, ada_b):
    d = D_MODEL
    return pl.pallas_call(
        _adaln_kernel,
        grid=(DEPTH, 6),
        in_specs=[
            _const_spec((BATCH, d)),
            pl.BlockSpec((None, d, d), lambda i, j: (i, 0, j)),
            pl.BlockSpec((None, 1, d), lambda i, j: (i, 0, j)),
        ],
        out_specs=pl.BlockSpec((None, None, BATCH, d), lambda i, j: (i, j, 0, 0)),
        out_shape=jax.ShapeDtypeStruct((DEPTH, 6, BATCH, d), F32),
        compiler_params=_params("arbitrary", "arbitrary"),
        name="adaln",
    )(c, ada_w, ada_b.reshape(DEPTH, 1, 6 * d))


def _row_spec():
    return pl.BlockSpec((None, ROW_TILE, D_MODEL), lambda b, l: (b, l, 0))


def _mod_spec(layer):
    return _layer_spec((6, BATCH, D_MODEL), layer)


def _mod_row(mod_ref, which):
    return mod_ref[which, pl.ds(pl.program_id(0), 1), :]


def _mlp_kernel(x_ref, mod_ref, g_ref, w1_ref, w2_ref, fg_ref, o_ref, *, layer, final):
    x = x_ref[...]
    h = _rmsnorm(x, g_ref[layer:layer + 1, :]) * (1.0 + _mod_row(mod_ref, SCALE2))
    h = (h + _mod_row(mod_ref, SHIFT2)).astype(BF16)
    acc = jnp.zeros((ROW_TILE, D_MODEL), F32)
    for k in range(D_FF // FF_TILE):
        hid = _dot(h, w1_ref[:, k * FF_TILE:(k + 1) * FF_TILE])
        hid = jnp.square(jnp.maximum(hid, 0.0)).astype(BF16)
        acc = acc + _dot(hid, w2_ref[k * FF_TILE:(k + 1) * FF_TILE, :])
    out = x + _mod_row(mod_ref, GATE2) * acc
    if final:
        out = _rmsnorm(out, fg_ref[...])
    o_ref[...] = out


def _mlp_layer(x, mods, norm_g, w1, w2, final_g, *, layer):
    d = D_MODEL
    return pl.pallas_call(
        functools.partial(_mlp_kernel, layer=layer, final=layer == DEPTH - 1),
        grid=(BATCH, SEQ // ROW_TILE),
        in_specs=[_row_spec(), _mod_spec(layer), _const_spec((DEPTH, d)),
                  _const_spec((d, D_FF)), _const_spec((D_FF, d)), _const_spec((1, d))],
        out_specs=_row_spec(),
        out_shape=jax.ShapeDtypeStruct((BATCH, SEQ, d), F32),
        compiler_params=_params("parallel", "arbitrary"),
        name="mlp",
    )(x, mods, norm_g, w1, w2, final_g)


def _conv_kernel(x_ref, mod_ref, g_ref, win_ref, cw_ref, cb_ref, wout_ref, *rest,
                 layer, idx, n_cast):
    casts, o_ref, (zbuf,) = _split_cast_refs(rest, n_cast)
    _cast_blocks(casts)
    d = D_MIX
    halo = SUBLANES

    @pl.when(pl.program_id(1) == 0)
    def _():
        zbuf[0:halo, :] = jnp.zeros((halo, d), F32)

    x = x_ref[...]
    h = _rmsnorm(x, g_ref[layer:layer + 1, :]) * (1.0 + _mod_row(mod_ref, SCALE1))
    h = (h + _mod_row(mod_ref, SHIFT1)).astype(BF16)
    bcx = _dot(h, win_ref[...])
    b_gate = bcx[:, 0:d]
    z = bcx[:, d:2 * d] * bcx[:, 2 * d:3 * d]
    zbuf[halo:halo + ROW_TILE, :] = z
    conv = (cw_ref[idx, 2:3, :] * z
            + cw_ref[idx, 1:2, :] * zbuf[halo - 1:halo - 1 + ROW_TILE, :]
            + cw_ref[idx, 0:1, :] * zbuf[halo - 2:halo - 2 + ROW_TILE, :]
            + cb_ref[idx:idx + 1, :])
    zbuf[0:halo, :] = zbuf[ROW_TILE:ROW_TILE + halo, :]
    y = _dot((b_gate * conv).astype(BF16), wout_ref[...])
    o_ref[...] = x + _mod_row(mod_ref, GATE1) * y


def _conv_layer(x, mods, norm_g, w_in, w_out, cw, cb, *, casts, layer, idx):
    d = D_MODEL
    n = cw.shape[0]
    grid = (BATCH, SEQ // ROW_TILE)
    c_in, c_out, c_shapes, c_args = _cast_plan(casts, grid[0] * grid[1], _tile_step)
    out, *cast_out = pl.pallas_call(
        functools.partial(_conv_kernel, layer=layer, idx=idx, n_cast=len(casts)),
        grid=grid,
        in_specs=[_row_spec(), _mod_spec(layer), _const_spec((DEPTH, d)),
                  _const_spec((d, 3 * D_MIX)), _const_spec((n, CONV_WIDTH, D_MIX)),
                  _const_spec((n, D_MIX)), _const_spec((D_MIX, d))] + c_in,
        out_specs=[_row_spec()] + c_out,
        out_shape=[jax.ShapeDtypeStruct((BATCH, SEQ, d), F32)] + c_shapes,
        scratch_shapes=[pltpu.VMEM((ROW_TILE + SUBLANES, D_MIX), F32)],
        compiler_params=_params("arbitrary", "arbitrary"),
        name="conv_mixer",
    )(x, mods, norm_g, w_in, cw, cb, w_out, *c_args)
    return out, cast_out


def _sg_kernel(x_ref, mod_ref, g_ref, win_ref, vg_ref, ws_ref, bs_ref, wout_ref, *rest,
               layer, idx, n_cast):
    casts, o_ref, (vm_ref,) = _split_cast_refs(rest, n_cast)
    _cast_blocks(casts)
    d = D_MIX
    x = x_ref[...]
    h = _rmsnorm(x, g_ref[layer:layer + 1, :]) * (1.0 + _mod_row(mod_ref, SCALE1))
    h = (h + _mod_row(mod_ref, SHIFT1)).astype(BF16)
    uv = _dot(h, win_ref[...])
    u = uv[:, 0:d]
    v = _rmsnorm(uv[:, d:2 * d], vg_ref[idx:idx + 1, :]).astype(BF16)
    row = lax.broadcasted_iota(jnp.int32, (CHUNK, CHUNK), 0)
    col = lax.broadcasted_iota(jnp.int32, (CHUNK, CHUNK), 1)
    causal = col <= row
    for hd in range(SG_HEADS):
        ws = jnp.where(causal, ws_ref[hd], 0.0).astype(BF16)
        bias = bs_ref[:, hd:hd + 1]
        lanes = slice(hd * SG_HEAD_DIM, (hd + 1) * SG_HEAD_DIM)
        for n in range(ROW_TILE // CHUNK):
            rows = slice(n * CHUNK, (n + 1) * CHUNK)
            vm_ref[rows, lanes] = _dot(ws, v[rows, lanes]) + bias
    y = _dot((u * vm_ref[...]).astype(BF16), wout_ref[...])
    o_ref[...] = x + _mod_row(mod_ref, GATE1) * y


def _sg_layer(x, mods, norm_g, w_in, w_out, v_g, w_s, b_s_t, *, casts, layer, idx):
    d = D_MODEL
    n = v_g.shape[0]
    grid = (BATCH, SEQ // ROW_TILE)
    c_in, c_out, c_shapes, c_args = _cast_plan(casts, grid[0] * grid[1], _tile_step)
    out, *cast_out = pl.pallas_call(
        functools.partial(_sg_kernel, layer=layer, idx=idx, n_cast=len(casts)),
        grid=grid,
        in_specs=[_row_spec(), _mod_spec(layer), _const_spec((DEPTH, d)),
                  _const_spec((d, 2 * D_MIX)), _const_spec((n, D_MIX)),
                  _layer_spec((SG_HEADS, CHUNK, CHUNK), idx), _layer_spec((CHUNK, SG_HEADS), idx),
                  _const_spec((D_MIX, d))] + c_in,
        out_specs=[_row_spec()] + c_out,
        out_shape=[jax.ShapeDtypeStruct((BATCH, SEQ, d), F32)] + c_shapes,
        scratch_shapes=[pltpu.VMEM((ROW_TILE, D_MIX), F32)],
        compiler_params=_params("arbitrary", "arbitrary"),
        name="sg_mixer",
    )(x, mods, norm_g, w_in, v_g, w_s, b_s_t, w_out, *c_args)
    return out, cast_out


def _replicate_block_diag(w, n_rep, row_block, col_block):
    rows = w.shape[0]
    src = lax.broadcasted_iota(jnp.int32, (col_block, n_rep * col_block), 0)
    dst = lax.broadcasted_iota(jnp.int32, (col_block, n_rep * col_block), 1)
    replicate = jnp.where(dst % col_block == src, 1.0, 0.0).astype(BF16)
    tiled = _dot(w.astype(BF16), replicate)
    r = lax.broadcasted_iota(jnp.int32, (rows, n_rep * col_block), 0)
    c = lax.broadcasted_iota(jnp.int32, (rows, n_rep * col_block), 1)
    return jnp.where(r // row_block == c // col_block, tiled, 0.0).astype(BF16)


def _s5_prep_kernel(are_ref, aim_ref, ldt_ref, bre_ref, bim_ref, cre_ref, cim_ref,
                    abar_re_ref, abar_im_ref, wbre_ref, wbim_ref, wcre_ref, wcim_ref):
    a_re = are_ref[...]
    a_im = aim_ref[...]
    dt = jnp.exp(ldt_ref[...])
    mag = jnp.exp(a_re * dt)
    abar_re = mag * jnp.cos(a_im * dt)
    abar_im = mag * jnp.sin(a_im * dt)
    den = a_re * a_re + a_im * a_im
    nr = abar_re - 1.0
    ni = abar_im
    f_re = ((nr * a_re + ni * a_im) / den)[:, None, :]
    f_im = ((ni * a_re - nr * a_im) / den)[:, None, :]
    b_re = bre_ref[...]
    b_im = bim_ref[...]
    abar_re_ref[...] = abar_re
    abar_im_ref[...] = abar_im
    bbar_re = (f_re * b_re - f_im * b_im).reshape(S5_GROUPS * S5_GROUP, S5_STATE)
    bbar_im = (f_re * b_im + f_im * b_re).reshape(S5_GROUPS * S5_GROUP, S5_STATE)
    for k in range(S5_NSUPER):
        rows_in = slice(k * MXU_DIM, (k + 1) * MXU_DIM)
        wbre_ref[k] = _replicate_block_diag(bbar_re[rows_in, :], S5_SUPER, S5_GROUP, S5_STATE)
        wbim_ref[k] = _replicate_block_diag(bbar_im[rows_in, :], S5_SUPER, S5_GROUP, S5_STATE)
        rows_out = slice(k * S5_SUPER_STATE, (k + 1) * S5_SUPER_STATE)
        wcre_ref[k] = _replicate_block_diag(cre_ref[rows_out, :], S5_SUPER, S5_STATE, S5_GROUP)
        wcim_ref[k] = _replicate_block_diag(cim_ref[rows_out, :], S5_SUPER, S5_STATE, S5_GROUP)


def _s5_prep(a_re, a_im, log_dt, b_re, b_im, c_re, c_im):
    g, p, h = S5_GROUPS, S5_STATE, S5_GROUP
    state_major = lambda c: c.transpose(0, 2, 1).reshape(g * p, h)
    return pl.pallas_call(
        _s5_prep_kernel,
        out_shape=[jax.ShapeDtypeStruct((g, p), F32), jax.ShapeDtypeStruct((g, p), F32),
                   jax.ShapeDtypeStruct((S5_NSUPER, MXU_DIM, S5_SUPER_STATE), BF16),
                   jax.ShapeDtypeStruct((S5_NSUPER, MXU_DIM, S5_SUPER_STATE), BF16),
                   jax.ShapeDtypeStruct((S5_NSUPER, S5_SUPER_STATE, MXU_DIM), BF16),
                   jax.ShapeDtypeStruct((S5_NSUPER, S5_SUPER_STATE, MXU_DIM), BF16)],
        name="s5_prep",
    )(a_re, a_im, log_dt.reshape(g, 1), b_re.transpose(0, 2, 1), b_im.transpose(0, 2, 1),
      state_major(c_re), state_major(c_im))


def _time_major_perm():
    r = lax.broadcasted_iota(jnp.int32, (MXU_DIM, MXU_DIM), 0)
    c = lax.broadcasted_iota(jnp.int32, (MXU_DIM, MXU_DIM), 1)
    return r, c, (r % BATCH) * S5_PERM_STEPS + r // BATCH


def _s5_kernel(x_ref, mod_ref, g_ref, win_ref, are_ref, aim_ref, wbre_ref, wbim_ref,
               wcre_ref, wcim_ref, dskip_ref, gluw_ref, glub_ref, wout_ref, *rest,
               layer, idx, n_cast):
    casts, o_ref, scratch = _split_cast_refs(rest, n_cast)
    h_ref, bre_ref, bim_ref, sre_ref, sim_ref, state_re, state_im, y_ref = scratch
    _cast_blocks(casts)
    d = D_MODEL

    @pl.when(pl.program_id(0) == 0)
    def _():
        state_re[...] = jnp.zeros_like(state_re)
        state_im[...] = jnp.zeros_like(state_im)

    r, c, src = _time_major_perm()
    to_time_major = jnp.where(c == src, 1.0, 0.0).astype(BF16)
    to_batch_major = jnp.where(r == (c % BATCH) * S5_PERM_STEPS + c // BATCH, 1.0, 0.0).astype(BF16)

    g = g_ref[layer:layer + 1, :]
    for part in range(S5_STEPS // S5_PERM_STEPS):
        steps = slice(part * S5_PERM_STEPS, (part + 1) * S5_PERM_STEPS)
        h_rows = []
        for b in range(BATCH):
            h_b = _rmsnorm(x_ref[b, steps, :], g) * (1.0 + mod_ref[SCALE1, b:b + 1, :])
            h_rows.append((h_b + mod_ref[SHIFT1, b:b + 1, :]).astype(BF16))
        h_part = jnp.concatenate(h_rows, axis=0)
        h_ref[part * MXU_DIM:(part + 1) * MXU_DIM, :] = _dot(to_time_major, h_part).astype(BF16)

    u = _dot(h_ref[...], win_ref[...])
    ub = u.astype(BF16)
    pair = 2 * BATCH
    for k in range(S5_NSUPER):
        buf = k % 2
        uk = ub[:, k * MXU_DIM:(k + 1) * MXU_DIM]
        bre_ref[buf] = _dot(uk, wbre_ref[k])
        bim_ref[buf] = _dot(uk, wbim_ref[k])
        blocks = []
        for c0 in range(0, S5_SUPER_STATE, SCAN_COLS):
            gcols = slice(k * S5_SUPER_STATE + c0, k * S5_SUPER_STATE + c0 + SCAN_COLS)
            blocks.append(dict(
                cols=slice(c0, c0 + SCAN_COLS), gcols=gcols,
                a_re=jnp.broadcast_to(are_ref[:, gcols], (BATCH, SCAN_COLS)),
                a_im=jnp.broadcast_to(aim_ref[:, gcols], (BATCH, SCAN_COLS)),
                s_re=state_re[:, gcols], s_im=state_im[:, gcols]))
        for t2 in range(S5_STEPS // 2):
            for blk in blocks:
                new_re, new_im = [], []
                for t in (2 * t2, 2 * t2 + 1):
                    rows = slice(t * BATCH, (t + 1) * BATCH)
                    s_re, s_im = blk["s_re"], blk["s_im"]
                    blk["s_re"] = (blk["a_re"] * s_re - blk["a_im"] * s_im
                                   + bre_ref[buf, rows, blk["cols"]])
                    blk["s_im"] = (blk["a_re"] * s_im + blk["a_im"] * s_re
                                   + bim_ref[buf, rows, blk["cols"]])
                    new_re.append(blk["s_re"])
                    new_im.append(blk["s_im"])
                rows2 = slice(t2 * pair, (t2 + 1) * pair)
                sre_ref[buf, rows2, blk["cols"]] = jnp.concatenate(new_re, axis=0).astype(BF16)
                sim_ref[buf, rows2, blk["cols"]] = jnp.concatenate(new_im, axis=0).astype(BF16)
        for blk in blocks:
            state_re[:, blk["gcols"]] = blk["s_re"]
            state_im[:, blk["gcols"]] = blk["s_im"]
        y_ref[:, k * MXU_DIM:(k + 1) * MXU_DIM] = (
            _dot(sre_ref[buf], wcre_ref[k]) - _dot(sim_ref[buf], wcim_ref[k]))

    y = jax.nn.gelu(y_ref[...] + dskip_ref[idx:idx + 1, :] * u)
    y = y * jax.nn.sigmoid(_dot(y.astype(BF16), gluw_ref[...]) + glub_ref[idx:idx + 1, :])
    yb = y.astype(BF16)

    for part in range(S5_STEPS // S5_PERM_STEPS):
        steps = slice(part * S5_PERM_STEPS, (part + 1) * S5_PERM_STEPS)
        y_part = _dot(to_batch_major, yb[part * MXU_DIM:(part + 1) * MXU_DIM, :]).astype(BF16)
        out = _dot(y_part, wout_ref[...])
        for b in range(BATCH):
            out_b = out[b * S5_PERM_STEPS:(b + 1) * S5_PERM_STEPS, :]
            o_ref[b, steps, :] = x_ref[b, steps, :] + mod_ref[GATE1, b:b + 1, :] * out_b


def _s5_layer(x, mods, norm_g, w_in, glu_w, w_out, abar_re, abar_im, wb_re, wb_im, wc_re, wc_im,
              d_skip, glu_b, *, casts, layer, idx):
    d = D_MODEL
    n = d_skip.shape[0]
    n_steps = SEQ // S5_STEPS
    x_spec = pl.BlockSpec((BATCH, S5_STEPS, d), lambda i: (0, i, 0))
    c_in, c_out, c_shapes, c_args = _cast_plan(casts, n_steps, lambda i: i)
    out, *cast_out = pl.pallas_call(
        functools.partial(_s5_kernel, layer=layer, idx=idx, n_cast=len(casts)),
        grid=(n_steps,),
        in_specs=[x_spec, _mod_spec(layer), _const_spec((DEPTH, d)),
                  _const_spec((d, D_MIX)),
                  _const_spec((1, S5_NSTATE)), _const_spec((1, S5_NSTATE)),
                  _const_spec((S5_NSUPER, MXU_DIM, S5_SUPER_STATE)),
                  _const_spec((S5_NSUPER, MXU_DIM, S5_SUPER_STATE)),
                  _const_spec((S5_NSUPER, S5_SUPER_STATE, MXU_DIM)),
                  _const_spec((S5_NSUPER, S5_SUPER_STATE, MXU_DIM)),
                  _const_spec((n, D_MIX)), _const_spec((D_MIX, D_MIX)),
                  _const_spec((n, D_MIX)), _const_spec((D_MIX, d))] + c_in,
        out_specs=[x_spec] + c_out,
        out_shape=[jax.ShapeDtypeStruct((BATCH, SEQ, d), F32)] + c_shapes,
        scratch_shapes=[pltpu.VMEM((S5_ROWS, d), BF16),
                        pltpu.VMEM((2, S5_ROWS, S5_SUPER_STATE), F32),
                        pltpu.VMEM((2, S5_ROWS, S5_SUPER_STATE), F32),
                        pltpu.VMEM((2, S5_ROWS, S5_SUPER_STATE), BF16),
                        pltpu.VMEM((2, S5_ROWS, S5_SUPER_STATE), BF16),
                        pltpu.VMEM((BATCH, S5_NSTATE), F32),
                        pltpu.VMEM((BATCH, S5_NSTATE), F32),
                        pltpu.VMEM((S5_ROWS, D_MIX), F32)],
        compiler_params=_params("arbitrary"),
        name="s5_mixer",
    )(x, mods, norm_g, w_in, abar_re, abar_im, wb_re, wb_im, wc_re, wc_im, d_skip,
      glu_w, glu_b, w_out, *c_args)
    return out, cast_out


def kernel(x, c, ada_w, ada_b, norm1_g, norm2_g, ff_w1, ff_w2, final_g, conv_w_in, conv_w, conv_b, conv_w_out, ssm_w_in, ssm_a_re, ssm_a_im, ssm_log_dt, ssm_b_re, ssm_b_im, ssm_c_re, ssm_c_im, ssm_d, ssm_glu_w, ssm_glu_b, ssm_w_out, sg_w_in, sg_v_g, sg_w_s, sg_b_s, sg_w_out):
    mods = _adaln(c, ada_w, ada_b)
    sg_b_s_t = sg_b_s.transpose(0, 2, 1)
    final_g = final_g.reshape(1, D_MODEL)

    def mixer_weights(i):
        j = i // N_MIXERS
        stacks = [(conv_w_in, conv_w_out), (ssm_w_in, ssm_glu_w, ssm_w_out), (sg_w_in, sg_w_out)]
        return [(w, j) for w in stacks[i % N_MIXERS]]

    w_mix = [w[j].astype(BF16) for w, j in mixer_weights(0)]
    for i in range(DEPTH):
        kind = i % N_MIXERS
        j = i // N_MIXERS
        casts = [(ff_w1, i), (ff_w2, i)] + (mixer_weights(i + 1) if i + 1 < DEPTH else [])
        if kind == 0:
            x, cast_out = _conv_layer(x, mods, norm1_g, *w_mix, conv_w, conv_b,
                                      casts=casts, layer=i, idx=j)
        elif kind == 1:
            abar_re, abar_im, wb_re, wb_im, wc_re, wc_im = _s5_prep(
                ssm_a_re[j], ssm_a_im[j], ssm_log_dt[j], ssm_b_re[j], ssm_b_im[j],
                ssm_c_re[j], ssm_c_im[j])
            x, cast_out = _s5_layer(x, mods, norm1_g, *w_mix,
                                    abar_re.reshape(1, S5_NSTATE), abar_im.reshape(1, S5_NSTATE),
                                    wb_re, wb_im, wc_re, wc_im, ssm_d, ssm_glu_b,
                                    casts=casts, layer=i, idx=j)
        else:
            x, cast_out = _sg_layer(x, mods, norm1_g, *w_mix, sg_v_g, sg_w_s, sg_b_s_t,
                                    casts=casts, layer=i, idx=j)
        w1, w2, *w_mix = cast_out
        x = _mlp_layer(x, mods, norm2_g, w1, w2, final_g, layer=i)
    return x
```

```python
import functools

import jax
import jax.numpy as jnp
from jax import lax
from jax.experimental import pallas as pl
from jax.experimental.pallas import tpu as pltpu

D_MODEL = 1024
BATCH = 8
SEQ = 4096
DEPTH = 4
N_MIXERS = 3
D_FF = 4 * D_MODEL
D_MIX = D_MODEL
EPS = 1e-6
CONV_WIDTH = 3
S5_GROUP = 16
S5_GROUPS = D_MIX // S5_GROUP
S5_STATE = 64
CHUNK = 128
SG_HEADS = 8
SG_HEAD_DIM = D_MIX // SG_HEADS

SUBLANES = 8
LANES = 128
MXU_DIM = 256

S5_SUPER = MXU_DIM // S5_GROUP
S5_NSUPER = S5_GROUPS // S5_SUPER
S5_SUPER_STATE = S5_SUPER * S5_STATE
S5_NSTATE = S5_GROUPS * S5_STATE

ROW_TILE = 1024
S5_ROWS = 512
S5_STEPS = S5_ROWS // BATCH
S5_PERM_STEPS = MXU_DIM // BATCH
FF_TILE = 1024
SCAN_COLS = 256
VMEM_LIMIT = 60 * 1024 * 1024

SHIFT1, SCALE1, GATE1, SHIFT2, SCALE2, GATE2 = range(6)

F32 = jnp.float32
BF16 = jnp.bfloat16


def _dot(a, b):
    return jnp.dot(a, b, preferred_element_type=F32)


def _rmsnorm(x, g):
    y = x * lax.rsqrt(jnp.mean(x * x, axis=-1, keepdims=True) + EPS)
    return y * g


def _const_spec(shape):
    zeros = (0,) * len(shape)
    return pl.BlockSpec(shape, lambda *_: zeros, pipeline_mode=pl.Buffered(1))


def _layer_spec(shape, layer):
    zeros = (0,) * len(shape)
    return pl.BlockSpec((None,) + tuple(shape), lambda *_: (layer,) + zeros,
                        pipeline_mode=pl.Buffered(1))


def _params(*semantics):
    return pltpu.CompilerParams(dimension_semantics=semantics, vmem_limit_bytes=VMEM_LIMIT)


BF16_SUBLANES = 16


def _cast_plan(casts, n_steps, step_index):
    in_specs, out_specs, out_shapes, args = [], [], [], []
    for stacked, entry in casts:
        _, rows, cols = stacked.shape
        block_rows, rem = divmod(rows, n_steps)
        assert rem == 0 and block_rows % BF16_SUBLANES == 0, (rows, n_steps)
        in_specs.append(pl.BlockSpec((None, block_rows, cols),
                                     lambda *g, entry=entry: (entry, step_index(*g), 0)))
        out_specs.append(pl.BlockSpec((block_rows, cols), lambda *g: (step_index(*g), 0)))
        out_shapes.append(jax.ShapeDtypeStruct((rows, cols), BF16))
        args.append(stacked)
    return in_specs, out_specs, out_shapes, args


def _split_cast_refs(rest, n_cast):
    cast_in = rest[:n_cast]
    o_ref = rest[n_cast]
    cast_out = rest[n_cast + 1:2 * n_cast + 1]
    return list(zip(cast_in, cast_out)), o_ref, rest[2 * n_cast + 1:]


def _cast_blocks(casts):
    for src, dst in casts:
        dst[...] = src[...].astype(BF16)


def _tile_step(b, l):
    return b * (SEQ // ROW_TILE) + l


def _adaln_kernel(c_ref, w_ref, b_ref, o_ref):
    c = c_ref[...]
    c_act = (c * jax.nn.sigmoid(c)).astype(BF16)
    o_ref[...] = _dot(c_act, w_ref[...].astype(BF16)) + b_ref[...]


def _adaln(c, ada_w, ada_b):
    d = D_MODEL
    return pl.pallas_call(
        _adaln_kernel,
        grid=(DEPTH, 6),
        in_specs=[
            _const_spec((BATCH, d)),
            pl.BlockSpec((None, d, d), lambda i, j: (i, 0, j)),
            pl.BlockSpec((None, 1, d), lambda i, j: (i, 0, j)),
        ],
        out_specs=pl.BlockSpec((None, None, BATCH, d), lambda i, j: (i, j, 0, 0)),
        out_shape=jax.ShapeDtypeStruct((DEPTH, 6, BATCH, d), F32),
        compiler_params=_params("arbitrary", "arbitrary"),
        name="adaln",
    )(c, ada_w, ada_b.reshape(DEPTH, 1, 6 * d))


def _row_spec():
    return pl.BlockSpec((None, ROW_TILE, D_MODEL), lambda b, l: (b, l, 0))


def _mod_spec(layer):
    return _layer_spec((6, BATCH, D_MODEL), layer)


def _mod_row(mod_ref, which):
    return mod_ref[which, pl.ds(pl.program_id(0), 1), :]


def _mlp_kernel(x_ref, mod_ref, g_ref, w1_ref, w2_ref, fg_ref, o_ref, *, layer, final):
    x = x_ref[...]
    h = _rmsnorm(x, g_ref[layer:layer + 1, :]) * (1.0 + _mod_row(mod_ref, SCALE2))
    h = (h + _mod_row(mod_ref, SHIFT2)).astype(BF16)
    acc = jnp.zeros((ROW_TILE, D_MODEL), F32)
    for k in range(D_FF // FF_TILE):
        hid = _dot(h, w1_ref[:, k * FF_TILE:(k + 1) * FF_TILE])
        hid = jnp.square(jnp.maximum(hid, 0.0)).astype(BF16)
        acc = acc + _dot(hid, w2_ref[k * FF_TILE:(k + 1) * FF_TILE, :])
    out = x + _mod_row(mod_ref, GATE2) * acc
    if final:
        out = _rmsnorm(out, fg_ref[...])
    o_ref[...] = out


def _mlp_layer(x, mods, norm_g, w1, w2, final_g, *, layer):
    d = D_MODEL
    return pl.pallas_call(
        functools.partial(_mlp_kernel, layer=layer, final=layer == DEPTH - 1),
        grid=(BATCH, SEQ // ROW_TILE),
        in_specs=[_row_spec(), _mod_spec(layer), _const_spec((DEPTH, d)),
                  _const_spec((d, D_FF)), _const_spec((D_FF, d)), _const_spec((1, d))],
        out_specs=_row_spec(),
        out_shape=jax.ShapeDtypeStruct((BATCH, SEQ, d), F32),
        compiler_params=_params("parallel", "arbitrary"),
        name="mlp",
    )(x, mods, norm_g, w1, w2, final_g)


def _conv_kernel(x_ref, mod_ref, g_ref, win_ref, cw_ref, cb_ref, wout_ref, *rest,
                 layer, idx, n_cast):
    casts, o_ref, (zbuf,) = _split_cast_refs(rest, n_cast)
    _cast_blocks(casts)
    d = D_MIX
    halo = SUBLANES

    @pl.when(pl.program_id(1) == 0)
    def _():
        zbuf[0:halo, :] = jnp.zeros((halo, d), F32)

    x = x_ref[...]
    h = _rmsnorm(x, g_ref[layer:layer + 1, :]) * (1.0 + _mod_row(mod_ref, SCALE1))
    h = (h + _mod_row(mod_ref, SHIFT1)).astype(BF16)
    bcx = _dot(h, win_ref[...])
    b_gate = bcx[:, 0:d]
    z = bcx[:, d:2 * d] * bcx[:, 2 * d:3 * d]
    zbuf[halo:halo + ROW_TILE, :] = z
    conv = (cw_ref[idx, 2:3, :] * z
            + cw_ref[idx, 1:2, :] * zbuf[halo - 1:halo - 1 + ROW_TILE, :]
            + cw_ref[idx, 0:1, :] * zbuf[halo - 2:halo - 2 + ROW_TILE, :]
            + cb_ref[idx:idx + 1, :])
    zbuf[0:halo, :] = zbuf[ROW_TILE:ROW_TILE + halo, :]
    y = _dot((b_gate * conv).astype(BF16), wout_ref[...])
    o_ref[...] = x + _mod_row(mod_ref, GATE1) * y


def _conv_layer(x, mods, norm_g, w_in, w_out, cw, cb, *, casts, layer, idx):
    d = D_MODEL
    n = cw.shape[0]
    grid = (BATCH, SEQ // ROW_TILE)
    c_in, c_out, c_shapes, c_args = _cast_plan(casts, grid[0] * grid[1], _tile_step)
    out, *cast_out = pl.pallas_call(
        functools.partial(_conv_kernel, layer=layer, idx=idx, n_cast=len(casts)),
        grid=grid,
        in_specs=[_row_spec(), _mod_spec(layer), _const_spec((DEPTH, d)),
                  _const_spec((d, 3 * D_MIX)), _const_spec((n, CONV_WIDTH, D_MIX)),
                  _const_spec((n, D_MIX)), _const_spec((D_MIX, d))] + c_in,
        out_specs=[_row_spec()] + c_out,
        out_shape=[jax.ShapeDtypeStruct((BATCH, SEQ, d), F32)] + c_shapes,
        scratch_shapes=[pltpu.VMEM((ROW_TILE + SUBLANES, D_MIX), F32)],
        compiler_params=_params("arbitrary", "arbitrary"),
        name="conv_mixer",
    )(x, mods, norm_g, w_in, cw, cb, w_out, *c_args)
    return out, cast_out


def _sg_kernel(x_ref, mod_ref, g_ref, win_ref, vg_ref, ws_ref, bs_ref, wout_ref, *rest,
               layer, idx, n_cast):
    casts, o_ref, (vm_ref,) = _split_cast_refs(rest, n_cast)
    _cast_blocks(casts)
    d = D_MIX
    x = x_ref[...]
    h = _rmsnorm(x, g_ref[layer:layer + 1, :]) * (1.0 + _mod_row(mod_ref, SCALE1))
    h = (h + _mod_row(mod_ref, SHIFT1)).astype(BF16)
    uv = _dot(h, win_ref[...])
    u = uv[:, 0:d]
    v = _rmsnorm(uv[:, d:2 * d], vg_ref[idx:idx + 1, :]).astype(BF16)
    row = lax.broadcasted_iota(jnp.int32, (CHUNK, CHUNK), 0)
    col = lax.broadcasted_iota(jnp.int32, (CHUNK, CHUNK), 1)
    causal = col <= row
    for hd in range(SG_HEADS):
        ws = jnp.where(causal, ws_ref[hd], 0.0).astype(BF16)
        bias = bs_ref[:, hd:hd + 1]
        lanes = slice(hd * SG_HEAD_DIM, (hd + 1) * SG_HEAD_DIM)
        for n in range(ROW_TILE // CHUNK):
            rows = slice(n * CHUNK, (n + 1) * CHUNK)
            vm_ref[rows, lanes] = _dot(ws, v[rows, lanes]) + bias
    y = _dot((u * vm_ref[...]).astype(BF16), wout_ref[...])
    o_ref[...] = x + _mod_row(mod_ref, GATE1) * y


def _sg_layer(x, mods, norm_g, w_in, w_out, v_g, w_s, b_s_t, *, casts, layer, idx):
    d = D_MODEL
    n = v_g.shape[0]
    grid = (BATCH, SEQ // ROW_TILE)
    c_in, c_out, c_shapes, c_args = _cast_plan(casts, grid[0] * grid[1], _tile_step)
    out, *cast_out = pl.pallas_call(
        functools.partial(_sg_kernel, layer=layer, idx=idx, n_cast=len(casts)),
        grid=grid,
        in_specs=[_row_spec(), _mod_spec(layer), _const_spec((DEPTH, d)),
                  _const_spec((d, 2 * D_MIX)), _const_spec((n, D_MIX)),
                  _layer_spec((SG_HEADS, CHUNK, CHUNK), idx), _layer_spec((CHUNK, SG_HEADS), idx),
                  _const_spec((D_MIX, d))] + c_in,
        out_specs=[_row_spec()] + c_out,
        out_shape=[jax.ShapeDtypeStruct((BATCH, SEQ, d), F32)] + c_shapes,
        scratch_shapes=[pltpu.VMEM((ROW_TILE, D_MIX), F32)],
        compiler_params=_params("arbitrary", "arbitrary"),
        name="sg_mixer",
    )(x, mods, norm_g, w_in, v_g, w_s, b_s_t, w_out, *c_args)
    return out, cast_out


def _replicate_block_diag(w, n_rep, row_block, col_block):
    rows = w.shape[0]
    src = lax.broadcasted_iota(jnp.int32, (col_block, n_rep * col_block), 0)
    dst = lax.broadcasted_iota(jnp.int32, (col_block, n_rep * col_block), 1)
    replicate = jnp.where(dst % col_block == src, 1.0, 0.0).astype(BF16)
    tiled = _dot(w.astype(BF16), replicate)
    r = lax.broadcasted_iota(jnp.int32, (rows, n_rep * col_block), 0)
    c = lax.broadcasted_iota(jnp.int32, (rows, n_rep * col_block), 1)
    return jnp.where(r // row_block == c // col_block, tiled, 0.0).astype(BF16)


def _s5_prep_kernel(are_ref, aim_ref, ldt_ref, bre_ref, bim_ref, cre_ref, cim_ref,
                    abar_re_ref, abar_im_ref, wbre_ref, wbim_ref, wcre_ref, wcim_ref):
    a_re = are_ref[...]
    a_im = aim_ref[...]
    dt = jnp.exp(ldt_ref[...])
    mag = jnp.exp(a_re * dt)
    abar_re = mag * jnp.cos(a_im * dt)
    abar_im = mag * jnp.sin(a_im * dt)
    den = a_re * a_re + a_im * a_im
    nr = abar_re - 1.0
    ni = abar_im
    f_re = ((nr * a_re + ni * a_im) / den)[:, None, :]
    f_im = ((ni * a_re - nr * a_im) / den)[:, None, :]
    b_re = bre_ref[...]
    b_im = bim_ref[...]
    abar_re_ref[...] = abar_re
    abar_im_ref[...] = abar_im
    bbar_re = (f_re * b_re - f_im * b_im).reshape(S5_GROUPS * S5_GROUP, S5_STATE)
    bbar_im = (f_re * b_im + f_im * b_re).reshape(S5_GROUPS * S5_GROUP, S5_STATE)
    for k in range(S5_NSUPER):
        rows_in = slice(k * MXU_DIM, (k + 1) * MXU_DIM)
        wbre_ref[k] = _replicate_block_diag(bbar_re[rows_in, :], S5_SUPER, S5_GROUP, S5_STATE)
        wbim_ref[k] = _replicate_block_diag(bbar_im[rows_in, :], S5_SUPER, S5_GROUP, S5_STATE)
        rows_out = slice(k * S5_SUPER_STATE, (k + 1) * S5_SUPER_STATE)
        wcre_ref[k] = _replicate_block_diag(cre_ref[rows_out, :], S5_SUPER, S5_STATE, S5_GROUP)
        wcim_ref[k] = _replicate_block_diag(cim_ref[rows_out, :], S5_SUPER, S5_STATE, S5_GROUP)


def _s5_prep(a_re, a_im, log_dt, b_re, b_im, c_re, c_im):
    g, p, h = S5_GROUPS, S5_STATE, S5_GROUP
    state_major = lambda c: c.transpose(0, 2, 1).reshape(g * p, h)
    return pl.pallas_call(
        _s5_prep_kernel,
        out_shape=[jax.ShapeDtypeStruct((g, p), F32), jax.ShapeDtypeStruct((g, p), F32),
                   jax.ShapeDtypeStruct((S5_NSUPER, MXU_DIM, S5_SUPER_STATE), BF16),
                   jax.ShapeDtypeStruct((S5_NSUPER, MXU_DIM, S5_SUPER_STATE), BF16),
                   jax.ShapeDtypeStruct((S5_NSUPER, S5_SUPER_STATE, MXU_DIM), BF16),
                   jax.ShapeDtypeStruct((S5_NSUPER, S5_SUPER_STATE, MXU_DIM), BF16)],
        name="s5_prep",
    )(a_re, a_im, log_dt.reshape(g, 1), b_re.transpose(0, 2, 1), b_im.transpose(0, 2, 1),
      state_major(c_re), state_major(c_im))


def _time_major_perm():
    r = lax.broadcasted_iota(jnp.int32, (MXU_DIM, MXU_DIM), 0)
    c = lax.broadcasted_iota(jnp.int32, (MXU_DIM, MXU_DIM), 1)
    return r, c, (r % BATCH) * S5_PERM_STEPS + r // BATCH


def _s5_kernel(x_ref, mod_ref, g_ref, win_ref, are_ref, aim_ref, wbre_ref, wbim_ref,
               wcre_ref, wcim_ref, dskip_ref, gluw_ref, glub_ref, wout_ref, *rest,
               layer, idx, n_cast):
    casts, o_ref, scratch = _split_cast_refs(rest, n_cast)
    bre_ref, bim_ref, state_re, state_im, y_ref = scratch
    _cast_blocks(casts)
    d = D_MODEL

    @pl.when(pl.program_id(0) == 0)
    def _():
        state_re[...] = jnp.zeros_like(state_re)
        state_im[...] = jnp.zeros_like(state_im)

    r, c, src = _time_major_perm()
    to_time_major = jnp.where(c == src, 1.0, 0.0).astype(BF16)
    to_batch_major = jnp.where(r == (c % BATCH) * S5_PERM_STEPS + c // BATCH, 1.0, 0.0).astype(BF16)

    g = g_ref[layer:layer + 1, :]
    u_parts = []
    for part in range(S5_STEPS // S5_PERM_STEPS):
        steps = slice(part * S5_PERM_STEPS, (part + 1) * S5_PERM_STEPS)
        h_rows = []
        for b in range(BATCH):
            h_b = _rmsnorm(x_ref[b, steps, :], g) * (1.0 + mod_ref[SCALE1, b:b + 1, :])
            h_rows.append((h_b + mod_ref[SHIFT1, b:b + 1, :]).astype(BF16))
        h_part = jnp.concatenate(h_rows, axis=0)
        h_tm = _dot(to_time_major, h_part).astype(BF16)
        u_parts.append(_dot(h_tm, win_ref[...]))
    u = jnp.concatenate(u_parts, axis=0)
    ub = u.astype(BF16)
    for k in range(S5_NSUPER):
        uk = ub[:, k * MXU_DIM:(k + 1) * MXU_DIM]
        y_k = None
        for c_pair in range(0, S5_SUPER_STATE, 2 * SCAN_COLS):
            blocks = []
            for c0 in (c_pair, c_pair + SCAN_COLS):
                cols = slice(c0, c0 + SCAN_COLS)
                gcols = slice(k * S5_SUPER_STATE + c0, k * S5_SUPER_STATE + c0 + SCAN_COLS)
                bre_ref[:, cols] = _dot(uk, wbre_ref[k, :, cols])
                bim_ref[:, cols] = _dot(uk, wbim_ref[k, :, cols])
                blocks.append(dict(
                    cols=cols, gcols=gcols,
                    a_re=jnp.broadcast_to(are_ref[:, gcols], (BATCH, SCAN_COLS)),
                    a_im=jnp.broadcast_to(aim_ref[:, gcols], (BATCH, SCAN_COLS)),
                    s_re=state_re[:, gcols], s_im=state_im[:, gcols]))
            for t in range(S5_STEPS):
                rows = slice(t * BATCH, (t + 1) * BATCH)
                for blk in blocks:
                    s_re, s_im, cols = blk["s_re"], blk["s_im"], blk["cols"]
                    blk["s_re"] = blk["a_re"] * s_re - blk["a_im"] * s_im + bre_ref[rows, cols]
                    blk["s_im"] = blk["a_re"] * s_im + blk["a_im"] * s_re + bim_ref[rows, cols]
                    bre_ref[rows, cols] = blk["s_re"]
                    bim_ref[rows, cols] = blk["s_im"]
            for blk in blocks:
                cols = blk["cols"]
                state_re[:, blk["gcols"]] = blk["s_re"]
                state_im[:, blk["gcols"]] = blk["s_im"]
                part = (_dot(bre_ref[:, cols].astype(BF16), wcre_ref[k, cols, :])
                        - _dot(bim_ref[:, cols].astype(BF16), wcim_ref[k, cols, :]))
                y_k = part if y_k is None else y_k + part
        y_ref[:, k * MXU_DIM:(k + 1) * MXU_DIM] = y_k

    y = jax.nn.gelu(y_ref[...] + dskip_ref[idx:idx + 1, :] * u)
    y = y * jax.nn.sigmoid(_dot(y.astype(BF16), gluw_ref[...]) + glub_ref[idx:idx + 1, :])
    yb = y.astype(BF16)

    for part in range(S5_STEPS // S5_PERM_STEPS):
        steps = slice(part * S5_PERM_STEPS, (part + 1) * S5_PERM_STEPS)
        y_part = _dot(to_batch_major, yb[part * MXU_DIM:(part + 1) * MXU_DIM, :]).astype(BF16)
        out = _dot(y_part, wout_ref[...])
        for b in range(BATCH):
            out_b = out[b * S5_PERM_STEPS:(b + 1) * S5_PERM_STEPS, :]
            o_ref[b, steps, :] = x_ref[b, steps, :] + mod_ref[GATE1, b:b + 1, :] * out_b


def _s5_layer(x, mods, norm_g, w_in, glu_w, w_out, abar_re, abar_im, wb_re, wb_im, wc_re, wc_im,
              d_skip, glu_b, *, casts, layer, idx):
    d = D_MODEL
    n = d_skip.shape[0]
    n_steps = SEQ // S5_STEPS
    x_spec = pl.BlockSpec((BATCH, S5_STEPS, d), lambda i: (0, i, 0))
    c_in, c_out, c_shapes, c_args = _cast_plan(casts, n_steps, lambda i: i)
    out, *cast_out = pl.pallas_call(
        functools.partial(_s5_kernel, layer=layer, idx=idx, n_cast=len(casts)),
        grid=(n_steps,),
        in_specs=[x_spec, _mod_spec(layer), _const_spec((DEPTH, d)),
                  _const_spec((d, D_MIX)),
                  _const_spec((1, S5_NSTATE)), _const_spec((1, S5_NSTATE)),
                  _const_spec((S5_NSUPER, MXU_DIM, S5_SUPER_STATE)),
                  _const_spec((S5_NSUPER, MXU_DIM, S5_SUPER_STATE)),
                  _const_spec((S5_NSUPER, S5_SUPER_STATE, MXU_DIM)),
                  _const_spec((S5_NSUPER, S5_SUPER_STATE, MXU_DIM)),
                  _const_spec((n, D_MIX)), _const_spec((D_MIX, D_MIX)),
                  _const_spec((n, D_MIX)), _const_spec((D_MIX, d))] + c_in,
        out_specs=[x_spec] + c_out,
        out_shape=[jax.ShapeDtypeStruct((BATCH, SEQ, d), F32)] + c_shapes,
        scratch_shapes=[pltpu.VMEM((S5_ROWS, S5_SUPER_STATE), F32),
                        pltpu.VMEM((S5_ROWS, S5_SUPER_STATE), F32),
                        pltpu.VMEM((BATCH, S5_NSTATE), F32),
                        pltpu.VMEM((BATCH, S5_NSTATE), F32),
                        pltpu.VMEM((S5_ROWS, D_MIX), F32)],
        compiler_params=_params("arbitrary"),
        name="s5_mixer",
    )(x, mods, norm_g, w_in, abar_re, abar_im, wb_re, wb_im, wc_re, wc_im, d_skip,
      glu_w, glu_b, w_out, *c_args)
    return out, cast_out


def kernel(x, c, ada_w, ada_b, norm1_g, norm2_g, ff_w1, ff_w2, final_g, conv_w_in, conv_w, conv_b, conv_w_out, ssm_w_in, ssm_a_re, ssm_a_im, ssm_log_dt, ssm_b_re, ssm_b_im, ssm_c_re, ssm_c_im, ssm_d, ssm_glu_w, ssm_glu_b, ssm_w_out, sg_w_in, sg_v_g, sg_w_s, sg_b_s, sg_w_out):
    mods = _adaln(c, ada_w, ada_b)
    sg_b_s_t = sg_b_s.transpose(0, 2, 1)
    final_g = final_g.reshape(1, D_MODEL)

    def mixer_weights(i):
        j = i // N_MIXERS
        stacks = [(conv_w_in, conv_w_out), (ssm_w_in, ssm_glu_w, ssm_w_out), (sg_w_in, sg_w_out)]
        return [(w, j) for w in stacks[i % N_MIXERS]]

    w_mix = [w[j].astype(BF16) for w, j in mixer_weights(0)]
    for i in range(DEPTH):
        kind = i % N_MIXERS
        j = i // N_MIXERS
        casts = [(ff_w1, i), (ff_w2, i)] + (mixer_weights(i + 1) if i + 1 < DEPTH else [])
        if kind == 0:
            x, cast_out = _conv_layer(x, mods, norm1_g, *w_mix, conv_w, conv_b,
                                      casts=casts, layer=i, idx=j)
        elif kind == 1:
            abar_re, abar_im, wb_re, wb_im, wc_re, wc_im = _s5_prep(
                ssm_a_re[j], ssm_a_im[j], ssm_log_dt[j], ssm_b_re[j], ssm_b_im[j],
                ssm_c_re[j], ssm_c_im[j])
            x, cast_out = _s5_layer(x, mods, norm1_g, *w_mix,
                                    abar_re.reshape(1, S5_NSTATE), abar_im.reshape(1, S5_NSTATE),
                                    wb_re, wb_im, wc_re, wc_im, ssm_d, ssm_glu_b,
                                    casts=casts, layer=i, idx=j)
        else:
            x, cast_out = _sg_layer(x, mods, norm1_g, *w_mix, sg_v_g, sg_w_s, sg_b_s_t,
                                    casts=casts, layer=i, idx=j)
        w1, w2, *w_mix = cast_out
        x = _mlp_layer(x, mods, norm2_g, w1, w2, final_g, layer=i)
    return x
```

```python
import functools

import jax
import jax.numpy as jnp
from jax import lax
from jax.experimental import pallas as pl
from jax.experimental.pallas import tpu as pltpu

D_MODEL = 1024
BATCH = 8
SEQ = 4096
DEPTH = 4
N_MIXERS = 3
D_FF = 4 * D_MODEL
D_MIX = D_MODEL
EPS = 1e-6
CONV_WIDTH = 3
S5_GROUP = 16
S5_GROUPS = D_MIX // S5_GROUP
S5_STATE = 64
CHUNK = 128
SG_HEADS = 8
SG_HEAD_DIM = D_MIX // SG_HEADS

SUBLANES = 8
LANES = 128
MXU_DIM = 256

S5_SUPER = MXU_DIM // S5_GROUP
S5_NSUPER = S5_GROUPS // S5_SUPER
S5_SUPER_STATE = S5_SUPER * S5_STATE
S5_NSTATE = S5_GROUPS * S5_STATE

ROW_TILE = 1024
S5_ROWS = 1024
S5_STEPS = S5_ROWS // BATCH
S5_PERM_STEPS = MXU_DIM // BATCH
FF_TILE = 1024
SCAN_COLS = 256
SCAN_INFLIGHT = 2
VMEM_LIMIT = 60 * 1024 * 1024

SHIFT1, SCALE1, GATE1, SHIFT2, SCALE2, GATE2 = range(6)

F32 = jnp.float32
BF16 = jnp.bfloat16


def _dot(a, b):
    return jnp.dot(a, b, preferred_element_type=F32)


def _rmsnorm(x, g):
    y = x * lax.rsqrt(jnp.mean(x * x, axis=-1, keepdims=True) + EPS)
    return y * g


def _const_spec(shape):
    zeros = (0,) * len(shape)
    return pl.BlockSpec(shape, lambda *_: zeros, pipeline_mode=pl.Buffered(1))


def _layer_spec(shape, layer):
    zeros = (0,) * len(shape)
    return pl.BlockSpec((None,) + tuple(shape), lambda *_: (layer,) + zeros,
                        pipeline_mode=pl.Buffered(1))


def _params(*semantics):
    return pltpu.CompilerParams(dimension_semantics=semantics, vmem_limit_bytes=VMEM_LIMIT)


BF16_SUBLANES = 16


def _cast_plan(casts, n_steps, step_index):
    in_specs, out_specs, out_shapes, args = [], [], [], []
    for stacked, entry in casts:
        _, rows, cols = stacked.shape
        block_rows, rem = divmod(rows, n_steps)
        assert rem == 0 and block_rows % BF16_SUBLANES == 0, (rows, n_steps)
        in_specs.append(pl.BlockSpec((None, block_rows, cols),
                                     lambda *g, entry=entry: (entry, step_index(*g), 0)))
        out_specs.append(pl.BlockSpec((block_rows, cols), lambda *g: (step_index(*g), 0)))
        out_shapes.append(jax.ShapeDtypeStruct((rows, cols), BF16))
        args.append(stacked)
    return in_specs, out_specs, out_shapes, args


def _split_cast_refs(rest, n_cast):
    cast_in = rest[:n_cast]
    o_ref = rest[n_cast]
    cast_out = rest[n_cast + 1:2 * n_cast + 1]
    return list(zip(cast_in, cast_out)), o_ref, rest[2 * n_cast + 1:]


def _cast_blocks(casts):
    for src, dst in casts:
        dst[...] = src[...].astype(BF16)


def _tile_step(b, l):
    return b * (SEQ // ROW_TILE) + l


ADALN_ROWS = 3


def _adaln_kernel(c_ref, w_ref, b_ref, o_ref):
    d = D_MODEL
    c = c_ref[...]
    c_act = (c * jax.nn.sigmoid(c)).astype(BF16)
    mod = _dot(c_act, w_ref[...].astype(BF16)) + b_ref[...]
    for j in range(ADALN_ROWS):
        o_ref[j] = mod[:, j * d:(j + 1) * d]


def _adaln(c, ada_w, ada_b):
    d = D_MODEL
    return pl.pallas_call(
        _adaln_kernel,
        grid=(DEPTH, 6 // ADALN_ROWS),
        in_specs=[
            _const_spec((BATCH, d)),
            pl.BlockSpec((None, d, ADALN_ROWS * d), lambda i, j: (i, 0, j)),
            pl.BlockSpec((None, 1, ADALN_ROWS * d), lambda i, j: (i, 0, j)),
        ],
        out_specs=pl.BlockSpec((None, ADALN_ROWS, BATCH, d), lambda i, j: (i, j, 0, 0)),
        out_shape=jax.ShapeDtypeStruct((DEPTH, 6, BATCH, d), F32),
        compiler_params=_params("arbitrary", "arbitrary"),
        name="adaln",
    )(c, ada_w, ada_b.reshape(DEPTH, 1, 6 * d))


def _row_spec():
    return pl.BlockSpec((None, ROW_TILE, D_MODEL), lambda b, l: (b, l, 0))


def _mod_spec(layer):
    return _layer_spec((6, BATCH, D_MODEL), layer)


def _mod_row(mod_ref, which):
    return mod_ref[which, pl.ds(pl.program_id(0), 1), :]


def _mlp_kernel(x_ref, mod_ref, g_ref, w1_ref, w2_ref, fg_ref, o_ref, *, layer, final):
    x = x_ref[...]
    h = _rmsnorm(x, g_ref[layer:layer + 1, :]) * (1.0 + _mod_row(mod_ref, SCALE2))
    h = (h + _mod_row(mod_ref, SHIFT2)).astype(BF16)
    acc = jnp.zeros((ROW_TILE, D_MODEL), F32)
    for k in range(D_FF // FF_TILE):
        hid = _dot(h, w1_ref[:, k * FF_TILE:(k + 1) * FF_TILE])
        hid = jnp.square(jnp.maximum(hid, 0.0)).astype(BF16)
        acc = acc + _dot(hid, w2_ref[k * FF_TILE:(k + 1) * FF_TILE, :])
    out = x + _mod_row(mod_ref, GATE2) * acc
    if final:
        out = _rmsnorm(out, fg_ref[...])
    o_ref[...] = out


def _mlp_layer(x, mods, norm_g, w1, w2, final_g, *, layer):
    d = D_MODEL
    return pl.pallas_call(
        functools.partial(_mlp_kernel, layer=layer, final=layer == DEPTH - 1),
        grid=(BATCH, SEQ // ROW_TILE),
        in_specs=[_row_spec(), _mod_spec(layer), _const_spec((DEPTH, d)),
                  _const_spec((d, D_FF)), _const_spec((D_FF, d)), _const_spec((1, d))],
        out_specs=_row_spec(),
        out_shape=jax.ShapeDtypeStruct((BATCH, SEQ, d), F32),
        compiler_params=_params("parallel", "arbitrary"),
        name="mlp",
    )(x, mods, norm_g, w1, w2, final_g)


def _conv_kernel(x_ref, mod_ref, g_ref, win_ref, cw_ref, cb_ref, wout_ref, *rest,
                 layer, idx, n_cast):
    casts, o_ref, (zbuf,) = _split_cast_refs(rest, n_cast)
    _cast_blocks(casts)
    d = D_MIX
    halo = SUBLANES

    @pl.when(pl.program_id(1) == 0)
    def _():
        zbuf[0:halo, :] = jnp.zeros((halo, d), F32)

    x = x_ref[...]
    h = _rmsnorm(x, g_ref[layer:layer + 1, :]) * (1.0 + _mod_row(mod_ref, SCALE1))
    h = (h + _mod_row(mod_ref, SHIFT1)).astype(BF16)
    bcx = _dot(h, win_ref[...])
    b_gate = bcx[:, 0:d]
    z = bcx[:, d:2 * d] * bcx[:, 2 * d:3 * d]
    zbuf[halo:halo + ROW_TILE, :] = z
    conv = (cw_ref[idx, 2:3, :] * z
            + cw_ref[idx, 1:2, :] * zbuf[halo - 1:halo - 1 + ROW_TILE, :]
            + cw_ref[idx, 0:1, :] * zbuf[halo - 2:halo - 2 + ROW_TILE, :]
            + cb_ref[idx:idx + 1, :])
    zbuf[0:halo, :] = zbuf[ROW_TILE:ROW_TILE + halo, :]
    y = _dot((b_gate * conv).astype(BF16), wout_ref[...])
    o_ref[...] = x + _mod_row(mod_ref, GATE1) * y


def _conv_layer(x, mods, norm_g, w_in, w_out, cw, cb, *, casts, layer, idx):
    d = D_MODEL
    n = cw.shape[0]
    grid = (BATCH, SEQ // ROW_TILE)
    c_in, c_out, c_shapes, c_args = _cast_plan(casts, grid[0] * grid[1], _tile_step)
    out, *cast_out = pl.pallas_call(
        functools.partial(_conv_kernel, layer=layer, idx=idx, n_cast=len(casts)),
        grid=grid,
        in_specs=[_row_spec(), _mod_spec(layer), _const_spec((DEPTH, d)),
                  _const_spec((d, 3 * D_MIX)), _const_spec((n, CONV_WIDTH, D_MIX)),
                  _const_spec((n, D_MIX)), _const_spec((D_MIX, d))] + c_in,
        out_specs=[_row_spec()] + c_out,
        out_shape=[jax.ShapeDtypeStruct((BATCH, SEQ, d), F32)] + c_shapes,
        scratch_shapes=[pltpu.VMEM((ROW_TILE + SUBLANES, D_MIX), F32)],
        compiler_params=_params("arbitrary", "arbitrary"),
        name="conv_mixer",
    )(x, mods, norm_g, w_in, cw, cb, w_out, *c_args)
    return out, cast_out


def _sg_kernel(x_ref, mod_ref, g_ref, win_ref, vg_ref, ws_ref, bs_ref, wout_ref, *rest,
               layer, idx, n_cast):
    casts, o_ref, (vm_ref,) = _split_cast_refs(rest, n_cast)
    _cast_blocks(casts)
    d = D_MIX
    x = x_ref[...]
    h = _rmsnorm(x, g_ref[layer:layer + 1, :]) * (1.0 + _mod_row(mod_ref, SCALE1))
    h = (h + _mod_row(mod_ref, SHIFT1)).astype(BF16)
    uv = _dot(h, win_ref[...])
    u = uv[:, 0:d]
    v = _rmsnorm(uv[:, d:2 * d], vg_ref[idx:idx + 1, :]).astype(BF16)
    row = lax.broadcasted_iota(jnp.int32, (CHUNK, CHUNK), 0)
    col = lax.broadcasted_iota(jnp.int32, (CHUNK, CHUNK), 1)
    causal = col <= row
    for hd in range(SG_HEADS):
        ws = jnp.where(causal, ws_ref[hd], 0.0).astype(BF16)
        bias = bs_ref[:, hd:hd + 1]
        lanes = slice(hd * SG_HEAD_DIM, (hd + 1) * SG_HEAD_DIM)
        for n in range(ROW_TILE // CHUNK):
            rows = slice(n * CHUNK, (n + 1) * CHUNK)
            vm_ref[rows, lanes] = _dot(ws, v[rows, lanes]) + bias
    y = _dot((u * vm_ref[...]).astype(BF16), wout_ref[...])
    o_ref[...] = x + _mod_row(mod_ref, GATE1) * y


def _sg_layer(x, mods, norm_g, w_in, w_out, v_g, w_s, b_s_t, *, casts, layer, idx):
    d = D_MODEL
    n = v_g.shape[0]
    grid = (BATCH, SEQ // ROW_TILE)
    c_in, c_out, c_shapes, c_args = _cast_plan(casts, grid[0] * grid[1], _tile_step)
    out, *cast_out = pl.pallas_call(
        functools.partial(_sg_kernel, layer=layer, idx=idx, n_cast=len(casts)),
        grid=grid,
        in_specs=[_row_spec(), _mod_spec(layer), _const_spec((DEPTH, d)),
                  _const_spec((d, 2 * D_MIX)), _const_spec((n, D_MIX)),
                  _layer_spec((SG_HEADS, CHUNK, CHUNK), idx), _layer_spec((CHUNK, SG_HEADS), idx),
                  _const_spec((D_MIX, d))] + c_in,
        out_specs=[_row_spec()] + c_out,
        out_shape=[jax.ShapeDtypeStruct((BATCH, SEQ, d), F32)] + c_shapes,
        scratch_shapes=[pltpu.VMEM((ROW_TILE, D_MIX), F32)],
        compiler_params=_params("arbitrary", "arbitrary"),
        name="sg_mixer",
    )(x, mods, norm_g, w_in, v_g, w_s, b_s_t, w_out, *c_args)
    return out, cast_out


def _replicate_block_diag(w, n_rep, row_block, col_block):
    rows = w.shape[0]
    src = lax.broadcasted_iota(jnp.int32, (col_block, n_rep * col_block), 0)
    dst = lax.broadcasted_iota(jnp.int32, (col_block, n_rep * col_block), 1)
    replicate = jnp.where(dst % col_block == src, 1.0, 0.0).astype(BF16)
    tiled = _dot(w.astype(BF16), replicate)
    r = lax.broadcasted_iota(jnp.int32, (rows, n_rep * col_block), 0)
    c = lax.broadcasted_iota(jnp.int32, (rows, n_rep * col_block), 1)
    return jnp.where(r // row_block == c // col_block, tiled, 0.0).astype(BF16)


def _s5_prep_kernel(are_ref, aim_ref, ldt_ref, bre_ref, bim_ref, cre_ref, cim_ref,
                    abar_re_ref, abar_im_ref, wbre_ref, wbim_ref, wcre_ref, wcim_ref):
    a_re = are_ref[...]
    a_im = aim_ref[...]
    dt = jnp.exp(ldt_ref[...])
    mag = jnp.exp(a_re * dt)
    abar_re = mag * jnp.cos(a_im * dt)
    abar_im = mag * jnp.sin(a_im * dt)
    den = a_re * a_re + a_im * a_im
    nr = abar_re - 1.0
    ni = abar_im
    f_re = ((nr * a_re + ni * a_im) / den)[:, None, :]
    f_im = ((ni * a_re - nr * a_im) / den)[:, None, :]
    b_re = bre_ref[...]
    b_im = bim_ref[...]
    abar_re_ref[...] = abar_re
    abar_im_ref[...] = abar_im
    bbar_re = (f_re * b_re - f_im * b_im).reshape(S5_GROUPS * S5_GROUP, S5_STATE)
    bbar_im = (f_re * b_im + f_im * b_re).reshape(S5_GROUPS * S5_GROUP, S5_STATE)
    for k in range(S5_NSUPER):
        rows_in = slice(k * MXU_DIM, (k + 1) * MXU_DIM)
        wbre_ref[k] = _replicate_block_diag(bbar_re[rows_in, :], S5_SUPER, S5_GROUP, S5_STATE)
        wbim_ref[k] = _replicate_block_diag(bbar_im[rows_in, :], S5_SUPER, S5_GROUP, S5_STATE)
        rows_out = slice(k * S5_SUPER_STATE, (k + 1) * S5_SUPER_STATE)
        wcre_ref[k] = _replicate_block_diag(cre_ref[rows_out, :], S5_SUPER, S5_STATE, S5_GROUP)
        wcim_ref[k] = _replicate_block_diag(cim_ref[rows_out, :], S5_SUPER, S5_STATE, S5_GROUP)


def _s5_prep(a_re, a_im, log_dt, b_re, b_im, c_re, c_im):
    g, p, h = S5_GROUPS, S5_STATE, S5_GROUP
    state_major = lambda c: c.transpose(0, 2, 1).reshape(g * p, h)
    return pl.pallas_call(
        _s5_prep_kernel,
        out_shape=[jax.ShapeDtypeStruct((g, p), F32), jax.ShapeDtypeStruct((g, p), F32),
                   jax.ShapeDtypeStruct((S5_NSUPER, MXU_DIM, S5_SUPER_STATE), BF16),
                   jax.ShapeDtypeStruct((S5_NSUPER, MXU_DIM, S5_SUPER_STATE), BF16),
                   jax.ShapeDtypeStruct((S5_NSUPER, S5_SUPER_STATE, MXU_DIM), BF16),
                   jax.ShapeDtypeStruct((S5_NSUPER, S5_SUPER_STATE, MXU_DIM), BF16)],
        name="s5_prep",
    )(a_re, a_im, log_dt.reshape(g, 1), b_re.transpose(0, 2, 1), b_im.transpose(0, 2, 1),
      state_major(c_re), state_major(c_im))


def _time_major_perm():
    r = lax.broadcasted_iota(jnp.int32, (MXU_DIM, MXU_DIM), 0)
    c = lax.broadcasted_iota(jnp.int32, (MXU_DIM, MXU_DIM), 1)
    return r, c, (r % BATCH) * S5_PERM_STEPS + r // BATCH


def _s5_kernel(x_ref, mod_ref, g_ref, win_ref, are_ref, aim_ref, wbre_ref, wbim_ref,
               wcre_ref, wcim_ref, dskip_ref, gluw_ref, glub_ref, wout_ref, *rest,
               layer, idx, n_cast):
    casts, o_ref, scratch = _split_cast_refs(rest, n_cast)
    bre_ref, bim_ref, state_re, state_im, y_ref = scratch
    _cast_blocks(casts)
    d = D_MODEL

    @pl.when(pl.program_id(0) == 0)
    def _():
        state_re[...] = jnp.zeros_like(state_re)
        state_im[...] = jnp.zeros_like(state_im)

    r, c, src = _time_major_perm()
    to_time_major = jnp.where(c == src, 1.0, 0.0).astype(BF16)
    to_batch_major = jnp.where(r == (c % BATCH) * S5_PERM_STEPS + c // BATCH, 1.0, 0.0).astype(BF16)

    g = g_ref[layer:layer + 1, :]
    u_parts = []
    for part in range(S5_STEPS // S5_PERM_STEPS):
        steps = slice(part * S5_PERM_STEPS, (part + 1) * S5_PERM_STEPS)
        h_rows = []
        for b in range(BATCH):
            h_b = _rmsnorm(x_ref[b, steps, :], g) * (1.0 + mod_ref[SCALE1, b:b + 1, :])
            h_rows.append((h_b + mod_ref[SHIFT1, b:b + 1, :]).astype(BF16))
        h_part = jnp.concatenate(h_rows, axis=0)
        h_tm = _dot(to_time_major, h_part).astype(BF16)
        u_parts.append(_dot(h_tm, win_ref[...]))
    u = jnp.concatenate(u_parts, axis=0)
    ub = u.astype(BF16)
    block_ids = [(k, c0) for k in range(S5_NSUPER) for c0 in range(0, S5_SUPER_STATE, SCAN_COLS)]
    y_parts = [None] * S5_NSUPER
    for first in range(0, len(block_ids), SCAN_INFLIGHT):
        blocks = []
        for k, c0 in block_ids[first:first + SCAN_INFLIGHT]:
            uk = ub[:, k * MXU_DIM:(k + 1) * MXU_DIM]
            cols = slice(c0, c0 + SCAN_COLS)
            gcols = slice(k * S5_SUPER_STATE + c0, k * S5_SUPER_STATE + c0 + SCAN_COLS)
            bre_ref[:, cols] = _dot(uk, wbre_ref[k, :, cols])
            bim_ref[:, cols] = _dot(uk, wbim_ref[k, :, cols])
            blocks.append(dict(
                k=k, cols=cols, gcols=gcols,
                a_re=jnp.broadcast_to(are_ref[:, gcols], (BATCH, SCAN_COLS)),
                a_im=jnp.broadcast_to(aim_ref[:, gcols], (BATCH, SCAN_COLS)),
                s_re=state_re[:, gcols], s_im=state_im[:, gcols]))
        for t in range(S5_STEPS):
            rows = slice(t * BATCH, (t + 1) * BATCH)
            for blk in blocks:
                s_re, s_im, cols = blk["s_re"], blk["s_im"], blk["cols"]
                blk["s_re"] = blk["a_re"] * s_re - blk["a_im"] * s_im + bre_ref[rows, cols]
                blk["s_im"] = blk["a_re"] * s_im + blk["a_im"] * s_re + bim_ref[rows, cols]
                bre_ref[rows, cols] = blk["s_re"]
                bim_ref[rows, cols] = blk["s_im"]
        for blk in blocks:
            k, cols = blk["k"], blk["cols"]
            state_re[:, blk["gcols"]] = blk["s_re"]
            state_im[:, blk["gcols"]] = blk["s_im"]
            part = (_dot(bre_ref[:, cols].astype(BF16), wcre_ref[k, cols, :])
                    - _dot(bim_ref[:, cols].astype(BF16), wcim_ref[k, cols, :]))
            y_parts[k] = part if y_parts[k] is None else y_parts[k] + part
    for k in range(S5_NSUPER):
        y_ref[:, k * MXU_DIM:(k + 1) * MXU_DIM] = y_parts[k]

    y = jax.nn.gelu(y_ref[...] + dskip_ref[idx:idx + 1, :] * u)
    y = y * jax.nn.sigmoid(_dot(y.astype(BF16), gluw_ref[...]) + glub_ref[idx:idx + 1, :])
    yb = y.astype(BF16)

    for part in range(S5_STEPS // S5_PERM_STEPS):
        steps = slice(part * S5_PERM_STEPS, (part + 1) * S5_PERM_STEPS)
        y_part = _dot(to_batch_major, yb[part * MXU_DIM:(part + 1) * MXU_DIM, :]).astype(BF16)
        out = _dot(y_part, wout_ref[...])
        for b in range(BATCH):
            out_b = out[b * S5_PERM_STEPS:(b + 1) * S5_PERM_STEPS, :]
            o_ref[b, steps, :] = x_ref[b, steps, :] + mod_ref[GATE1, b:b + 1, :] * out_b


def _s5_layer(x, mods, norm_g, w_in, glu_w, w_out, abar_re, abar_im, wb_re, wb_im, wc_re, wc_im,
              d_skip, glu_b, *, casts, layer, idx):
    d = D_MODEL
    n = d_skip.shape[0]
    n_steps = SEQ // S5_STEPS
    x_spec = pl.BlockSpec((BATCH, S5_STEPS, d), lambda i: (0, i, 0))
    c_in, c_out, c_shapes, c_args = _cast_plan(casts, n_steps, lambda i: i)
    out, *cast_out = pl.pallas_call(
        functools.partial(_s5_kernel, layer=layer, idx=idx, n_cast=len(casts)),
        grid=(n_steps,),
        in_specs=[x_spec, _mod_spec(layer), _const_spec((DEPTH, d)),
                  _const_spec((d, D_MIX)),
                  _const_spec((1, S5_NSTATE)), _const_spec((1, S5_NSTATE)),
                  _const_spec((S5_NSUPER, MXU_DIM, S5_SUPER_STATE)),
                  _const_spec((S5_NSUPER, MXU_DIM, S5_SUPER_STATE)),
                  _const_spec((S5_NSUPER, S5_SUPER_STATE, MXU_DIM)),
                  _const_spec((S5_NSUPER, S5_SUPER_STATE, MXU_DIM)),
                  _const_spec((n, D_MIX)), _const_spec((D_MIX, D_MIX)),
                  _const_spec((n, D_MIX)), _const_spec((D_MIX, d))] + c_in,
        out_specs=[x_spec] + c_out,
        out_shape=[jax.ShapeDtypeStruct((BATCH, SEQ, d), F32)] + c_shapes,
        scratch_shapes=[pltpu.VMEM((S5_ROWS, S5_SUPER_STATE), F32),
                        pltpu.VMEM((S5_ROWS, S5_SUPER_STATE), F32),
                        pltpu.VMEM((BATCH, S5_NSTATE), F32),
                        pltpu.VMEM((BATCH, S5_NSTATE), F32),
                        pltpu.VMEM((S5_ROWS, D_MIX), F32)],
        compiler_params=_params("arbitrary"),
        name="s5_mixer",
    )(x, mods, norm_g, w_in, abar_re, abar_im, wb_re, wb_im, wc_re, wc_im, d_skip,
      glu_w, glu_b, w_out, *c_args)
    return out, cast_out


def kernel(x, c, ada_w, ada_b, norm1_g, norm2_g, ff_w1, ff_w2, final_g, conv_w_in, conv_w, conv_b, conv_w_out, ssm_w_in, ssm_a_re, ssm_a_im, ssm_log_dt, ssm_b_re, ssm_b_im, ssm_c_re, ssm_c_im, ssm_d, ssm_glu_w, ssm_glu_b, ssm_w_out, sg_w_in, sg_v_g, sg_w_s, sg_b_s, sg_w_out):
    mods = _adaln(c, ada_w, ada_b)
    sg_b_s_t = sg_b_s.transpose(0, 2, 1)
    final_g = final_g.reshape(1, D_MODEL)

    def mixer_weights(i):
        j = i // N_MIXERS
        stacks = [(conv_w_in, conv_w_out), (ssm_w_in, ssm_glu_w, ssm_w_out), (sg_w_in, sg_w_out)]
        return [(w, j) for w in stacks[i % N_MIXERS]]

    w_mix = [w[j].astype(BF16) for w, j in mixer_weights(0)]
    for i in range(DEPTH):
        kind = i % N_MIXERS
        j = i // N_MIXERS
        casts = [(ff_w1, i), (ff_w2, i)] + (mixer_weights(i + 1) if i + 1 < DEPTH else [])
        if kind == 0:
            x, cast_out = _conv_layer(x, mods, norm1_g, *w_mix, conv_w, conv_b,
                                      casts=casts, layer=i, idx=j)
        elif kind == 1:
            abar_re, abar_im, wb_re, wb_im, wc_re, wc_im = _s5_prep(
                ssm_a_re[j], ssm_a_im[j], ssm_log_dt[j], ssm_b_re[j], ssm_b_im[j],
                ssm_c_re[j], ssm_c_im[j])
            x, cast_out = _s5_layer(x, mods, norm1_g, *w_mix,
                                    abar_re.reshape(1, S5_NSTATE), abar_im.reshape(1, S5_NSTATE),
                                    wb_re, wb_im, wc_re, wc_im, ssm_d, ssm_glu_b,
                                    casts=casts, layer=i, idx=j)
        else:
            x, cast_out = _sg_layer(x, mods, norm1_g, *w_mix, sg_v_g, sg_w_s, sg_b_s_t,
                                    casts=casts, layer=i, idx=j)
        w1, w2, *w_mix = cast_out
        x = _mlp_layer(x, mods, norm2_g, w1, w2, final_g, layer=i)
    return x
```

```python
import functools

import jax
import jax.numpy as jnp
from jax import lax
from jax.experimental import pallas as pl
from jax.experimental.pallas import tpu as pltpu

D_MODEL = 1024
BATCH = 8
SEQ = 4096
DEPTH = 4
N_MIXERS = 3
D_FF = 4 * D_MODEL
D_MIX = D_MODEL
EPS = 1e-6
CONV_WIDTH = 3
S5_GROUP = 16
S5_GROUPS = D_MIX // S5_GROUP
S5_STATE = 64
CHUNK = 128
SG_HEADS = 8
SG_HEAD_DIM = D_MIX // SG_HEADS

SUBLANES = 8
MXU_DIM = 256

S5_SUPER = MXU_DIM // S5_GROUP
S5_NSUPER = S5_GROUPS // S5_SUPER
S5_SUPER_STATE = S5_SUPER * S5_STATE
S5_NSTATE = S5_GROUPS * S5_STATE

ROW_TILE = 1024
S5_ROWS = 1024
S5_STEPS = S5_ROWS // BATCH
S5_PERM_STEPS = MXU_DIM // BATCH
FF_TILE = 1024
SCAN_COLS = 256
SCAN_INFLIGHT = 2
VMEM_LIMIT = 60 * 1024 * 1024

N_MOD = 6
SHIFT1, SCALE1, GATE1, SHIFT2, SCALE2, GATE2 = range(N_MOD)

F32 = jnp.float32
BF16 = jnp.bfloat16


def _dot(a, b):
    return jnp.dot(a, b, preferred_element_type=F32)


def _rmsnorm(x, g):
    y = x * lax.rsqrt(jnp.mean(x * x, axis=-1, keepdims=True) + EPS)
    return y * g


def _const_spec(shape):
    zeros = (0,) * len(shape)
    return pl.BlockSpec(shape, lambda *_: zeros, pipeline_mode=pl.Buffered(1))


def _layer_spec(shape, layer):
    zeros = (0,) * len(shape)
    return pl.BlockSpec((None,) + tuple(shape), lambda *_: (layer,) + zeros,
                        pipeline_mode=pl.Buffered(1))


def _params(*semantics):
    return pltpu.CompilerParams(dimension_semantics=semantics, vmem_limit_bytes=VMEM_LIMIT)


BF16_SUBLANES = 16


def _cast_plan(casts, n_steps, step_index):
    in_specs, out_specs, out_shapes, args = [], [], [], []
    for stacked, entry in casts:
        _, rows, cols = stacked.shape
        block_rows, rem = divmod(rows, n_steps)
        assert rem == 0 and block_rows % BF16_SUBLANES == 0, (rows, n_steps)
        in_specs.append(pl.BlockSpec((None, block_rows, cols),
                                     lambda *g, entry=entry: (entry, step_index(*g), 0)))
        out_specs.append(pl.BlockSpec((block_rows, cols), lambda *g: (step_index(*g), 0)))
        out_shapes.append(jax.ShapeDtypeStruct((rows, cols), BF16))
        args.append(stacked)
    return in_specs, out_specs, out_shapes, args


def _split_cast_refs(rest, n_cast):
    cast_in = rest[:n_cast]
    o_ref = rest[n_cast]
    cast_out = rest[n_cast + 1:2 * n_cast + 1]
    return list(zip(cast_in, cast_out)), o_ref, rest[2 * n_cast + 1:]


def _cast_blocks(casts):
    for src, dst in casts:
        dst[...] = src[...].astype(BF16)


def _tile_step(b, l):
    return b * (SEQ // ROW_TILE) + l


ADALN_ROWS = 3


def _adaln_kernel(c_ref, w_ref, b_ref, o_ref):
    d = D_MODEL
    c = c_ref[...]
    c_act = (c * jax.nn.sigmoid(c)).astype(BF16)
    mod = _dot(c_act, w_ref[...].astype(BF16)) + b_ref[...]
    for j in range(ADALN_ROWS):
        o_ref[j] = mod[:, j * d:(j + 1) * d]


def _adaln(c, ada_w, ada_b):
    d = D_MODEL
    return pl.pallas_call(
        _adaln_kernel,
        grid=(DEPTH, N_MOD // ADALN_ROWS),
        in_specs=[
            _const_spec((BATCH, d)),
            pl.BlockSpec((None, d, ADALN_ROWS * d), lambda i, j: (i, 0, j)),
            pl.BlockSpec((None, 1, ADALN_ROWS * d), lambda i, j: (i, 0, j)),
        ],
        out_specs=pl.BlockSpec((None, ADALN_ROWS, BATCH, d), lambda i, j: (i, j, 0, 0)),
        out_shape=jax.ShapeDtypeStruct((DEPTH, N_MOD, BATCH, d), F32),
        compiler_params=_params("arbitrary", "arbitrary"),
        name="adaln",
    )(c, ada_w, ada_b.reshape(DEPTH, 1, N_MOD * d))


def _row_spec():
    return pl.BlockSpec((None, ROW_TILE, D_MODEL), lambda b, l: (b, l, 0))


def _mod_spec(layer):
    return _layer_spec((N_MOD, BATCH, D_MODEL), layer)


def _mod_row(mod_ref, which):
    return mod_ref[which, pl.ds(pl.program_id(0), 1), :]


def _mlp_kernel(x_ref, mod_ref, g_ref, w1_ref, w2_ref, fg_ref, o_ref, *, layer, final):
    x = x_ref[...]
    h = _rmsnorm(x, g_ref[layer:layer + 1, :]) * (1.0 + _mod_row(mod_ref, SCALE2))
    h = (h + _mod_row(mod_ref, SHIFT2)).astype(BF16)
    acc = jnp.zeros((ROW_TILE, D_MODEL), F32)
    for k in range(D_FF // FF_TILE):
        hid = _dot(h, w1_ref[:, k * FF_TILE:(k + 1) * FF_TILE])
        hid = jnp.square(jnp.maximum(hid, 0.0)).astype(BF16)
        acc = acc + _dot(hid, w2_ref[k * FF_TILE:(k + 1) * FF_TILE, :])
    out = x + _mod_row(mod_ref, GATE2) * acc
    if final:
        out = _rmsnorm(out, fg_ref[...])
    o_ref[...] = out


def _mlp_layer(x, mods, norm_g, w1, w2, final_g, *, layer):
    d = D_MODEL
    return pl.pallas_call(
        functools.partial(_mlp_kernel, layer=layer, final=layer == DEPTH - 1),
        grid=(BATCH, SEQ // ROW_TILE),
        in_specs=[_row_spec(), _mod_spec(layer), _const_spec((DEPTH, d)),
                  _const_spec((d, D_FF)), _const_spec((D_FF, d)), _const_spec((1, d))],
        out_specs=_row_spec(),
        out_shape=jax.ShapeDtypeStruct((BATCH, SEQ, d), F32),
        compiler_params=_params("parallel", "arbitrary"),
        name="mlp",
    )(x, mods, norm_g, w1, w2, final_g)


def _conv_kernel(x_ref, mod_ref, g_ref, win_ref, cw_ref, cb_ref, wout_ref, *rest,
                 layer, idx, n_cast):
    casts, o_ref, (zbuf,) = _split_cast_refs(rest, n_cast)
    _cast_blocks(casts)
    d = D_MIX
    halo = SUBLANES

    @pl.when(pl.program_id(1) == 0)
    def _():
        zbuf[0:halo, :] = jnp.zeros((halo, d), F32)

    x = x_ref[...]
    h = _rmsnorm(x, g_ref[layer:layer + 1, :]) * (1.0 + _mod_row(mod_ref, SCALE1))
    h = (h + _mod_row(mod_ref, SHIFT1)).astype(BF16)
    bcx = _dot(h, win_ref[...])
    b_gate = bcx[:, 0:d]
    z = bcx[:, d:2 * d] * bcx[:, 2 * d:3 * d]
    zbuf[halo:halo + ROW_TILE, :] = z
    conv = (cw_ref[idx, 2:3, :] * z
            + cw_ref[idx, 1:2, :] * zbuf[halo - 1:halo - 1 + ROW_TILE, :]
            + cw_ref[idx, 0:1, :] * zbuf[halo - 2:halo - 2 + ROW_TILE, :]
            + cb_ref[idx:idx + 1, :])
    zbuf[0:halo, :] = zbuf[ROW_TILE:ROW_TILE + halo, :]
    y = _dot((b_gate * conv).astype(BF16), wout_ref[...])
    o_ref[...] = x + _mod_row(mod_ref, GATE1) * y


def _conv_layer(x, mods, norm_g, w_in, w_out, cw, cb, *, casts, layer, idx):
    d = D_MODEL
    n = cw.shape[0]
    grid = (BATCH, SEQ // ROW_TILE)
    c_in, c_out, c_shapes, c_args = _cast_plan(casts, grid[0] * grid[1], _tile_step)
    out, *cast_out = pl.pallas_call(
        functools.partial(_conv_kernel, layer=layer, idx=idx, n_cast=len(casts)),
        grid=grid,
        in_specs=[_row_spec(), _mod_spec(layer), _const_spec((DEPTH, d)),
                  _const_spec((d, 3 * D_MIX)), _const_spec((n, CONV_WIDTH, D_MIX)),
                  _const_spec((n, D_MIX)), _const_spec((D_MIX, d))] + c_in,
        out_specs=[_row_spec()] + c_out,
        out_shape=[jax.ShapeDtypeStruct((BATCH, SEQ, d), F32)] + c_shapes,
        scratch_shapes=[pltpu.VMEM((ROW_TILE + SUBLANES, D_MIX), F32)],
        compiler_params=_params("arbitrary", "arbitrary"),
        name="conv_mixer",
    )(x, mods, norm_g, w_in, cw, cb, w_out, *c_args)
    return out, cast_out


def _sg_kernel(x_ref, mod_ref, g_ref, win_ref, vg_ref, ws_ref, bs_ref, wout_ref, *rest,
               layer, idx, n_cast):
    casts, o_ref, (vm_ref,) = _split_cast_refs(rest, n_cast)
    _cast_blocks(casts)
    d = D_MIX
    x = x_ref[...]
    h = _rmsnorm(x, g_ref[layer:layer + 1, :]) * (1.0 + _mod_row(mod_ref, SCALE1))
    h = (h + _mod_row(mod_ref, SHIFT1)).astype(BF16)
    uv = _dot(h, win_ref[...])
    u = uv[:, 0:d]
    v = _rmsnorm(uv[:, d:2 * d], vg_ref[idx:idx + 1, :]).astype(BF16)
    row = lax.broadcasted_iota(jnp.int32, (CHUNK, CHUNK), 0)
    col = lax.broadcasted_iota(jnp.int32, (CHUNK, CHUNK), 1)
    causal = col <= row
    for hd in range(SG_HEADS):
        ws = jnp.where(causal, ws_ref[hd], 0.0).astype(BF16)
        bias = bs_ref[:, hd:hd + 1]
        lanes = slice(hd * SG_HEAD_DIM, (hd + 1) * SG_HEAD_DIM)
        for n in range(ROW_TILE // CHUNK):
            rows = slice(n * CHUNK, (n + 1) * CHUNK)
            vm_ref[rows, lanes] = _dot(ws, v[rows, lanes]) + bias
    y = _dot((u * vm_ref[...]).astype(BF16), wout_ref[...])
    o_ref[...] = x + _mod_row(mod_ref, GATE1) * y


def _sg_layer(x, mods, norm_g, w_in, w_out, v_g, w_s, b_s_t, *, casts, layer, idx):
    d = D_MODEL
    n = v_g.shape[0]
    grid = (BATCH, SEQ // ROW_TILE)
    c_in, c_out, c_shapes, c_args = _cast_plan(casts, grid[0] * grid[1], _tile_step)
    out, *cast_out = pl.pallas_call(
        functools.partial(_sg_kernel, layer=layer, idx=idx, n_cast=len(casts)),
        grid=grid,
        in_specs=[_row_spec(), _mod_spec(layer), _const_spec((DEPTH, d)),
                  _const_spec((d, 2 * D_MIX)), _const_spec((n, D_MIX)),
                  _layer_spec((SG_HEADS, CHUNK, CHUNK), idx), _layer_spec((CHUNK, SG_HEADS), idx),
                  _const_spec((D_MIX, d))] + c_in,
        out_specs=[_row_spec()] + c_out,
        out_shape=[jax.ShapeDtypeStruct((BATCH, SEQ, d), F32)] + c_shapes,
        scratch_shapes=[pltpu.VMEM((ROW_TILE, D_MIX), F32)],
        compiler_params=_params("arbitrary", "arbitrary"),
        name="sg_mixer",
    )(x, mods, norm_g, w_in, v_g, w_s, b_s_t, w_out, *c_args)
    return out, cast_out


def _replicate_block_diag(w, n_rep, row_block, col_block):
    rows = w.shape[0]
    src = lax.broadcasted_iota(jnp.int32, (col_block, n_rep * col_block), 0)
    dst = lax.broadcasted_iota(jnp.int32, (col_block, n_rep * col_block), 1)
    replicate = jnp.where(dst % col_block == src, 1.0, 0.0).astype(BF16)
    tiled = _dot(w.astype(BF16), replicate)
    r = lax.broadcasted_iota(jnp.int32, (rows, n_rep * col_block), 0)
    c = lax.broadcasted_iota(jnp.int32, (rows, n_rep * col_block), 1)
    return jnp.where(r // row_block == c // col_block, tiled, 0.0).astype(BF16)


def _s5_prep_kernel(are_ref, aim_ref, ldt_ref, bre_ref, bim_ref, cre_ref, cim_ref,
                    abar_re_ref, abar_im_ref, wbre_ref, wbim_ref, wcre_ref, wcim_ref):
    a_re = are_ref[...]
    a_im = aim_ref[...]
    dt = jnp.exp(ldt_ref[...])
    mag = jnp.exp(a_re * dt)
    abar_re = mag * jnp.cos(a_im * dt)
    abar_im = mag * jnp.sin(a_im * dt)
    den = a_re * a_re + a_im * a_im
    nr = abar_re - 1.0
    ni = abar_im
    f_re = ((nr * a_re + ni * a_im) / den)[:, None, :]
    f_im = ((ni * a_re - nr * a_im) / den)[:, None, :]
    b_re = bre_ref[...]
    b_im = bim_ref[...]
    abar_re_ref[...] = abar_re
    abar_im_ref[...] = abar_im
    bbar_re = (f_re * b_re - f_im * b_im).reshape(S5_GROUPS * S5_GROUP, S5_STATE)
    bbar_im = (f_re * b_im + f_im * b_re).reshape(S5_GROUPS * S5_GROUP, S5_STATE)
    for k in range(S5_NSUPER):
        rows_in = slice(k * MXU_DIM, (k + 1) * MXU_DIM)
        wbre_ref[k] = _replicate_block_diag(bbar_re[rows_in, :], S5_SUPER, S5_GROUP, S5_STATE)
        wbim_ref[k] = _replicate_block_diag(bbar_im[rows_in, :], S5_SUPER, S5_GROUP, S5_STATE)
        rows_out = slice(k * S5_SUPER_STATE, (k + 1) * S5_SUPER_STATE)
        wcre_ref[k] = _replicate_block_diag(cre_ref[rows_out, :], S5_SUPER, S5_STATE, S5_GROUP)
        wcim_ref[k] = _replicate_block_diag(cim_ref[rows_out, :], S5_SUPER, S5_STATE, S5_GROUP)


def _s5_prep(a_re, a_im, log_dt, b_re, b_im, c_re, c_im):
    g, p, h = S5_GROUPS, S5_STATE, S5_GROUP
    state_major = lambda c: c.transpose(0, 2, 1).reshape(g * p, h)
    return pl.pallas_call(
        _s5_prep_kernel,
        out_shape=[jax.ShapeDtypeStruct((g, p), F32), jax.ShapeDtypeStruct((g, p), F32),
                   jax.ShapeDtypeStruct((S5_NSUPER, MXU_DIM, S5_SUPER_STATE), BF16),
                   jax.ShapeDtypeStruct((S5_NSUPER, MXU_DIM, S5_SUPER_STATE), BF16),
                   jax.ShapeDtypeStruct((S5_NSUPER, S5_SUPER_STATE, MXU_DIM), BF16),
                   jax.ShapeDtypeStruct((S5_NSUPER, S5_SUPER_STATE, MXU_DIM), BF16)],
        name="s5_prep",
    )(a_re, a_im, log_dt.reshape(g, 1), b_re.transpose(0, 2, 1), b_im.transpose(0, 2, 1),
      state_major(c_re), state_major(c_im))


def _time_major_perm():
    r = lax.broadcasted_iota(jnp.int32, (MXU_DIM, MXU_DIM), 0)
    c = lax.broadcasted_iota(jnp.int32, (MXU_DIM, MXU_DIM), 1)
    return r, c, (r % BATCH) * S5_PERM_STEPS + r // BATCH


def _s5_kernel(x_ref, mod_ref, g_ref, win_ref, are_ref, aim_ref, wbre_ref, wbim_ref,
               wcre_ref, wcim_ref, dskip_ref, gluw_ref, glub_ref, wout_ref, *rest,
               layer, idx, n_cast):
    casts, o_ref, scratch = _split_cast_refs(rest, n_cast)
    bre_ref, bim_ref, state_re, state_im, y_ref = scratch
    _cast_blocks(casts)
    d = D_MODEL

    @pl.when(pl.program_id(0) == 0)
    def _():
        state_re[...] = jnp.zeros_like(state_re)
        state_im[...] = jnp.zeros_like(state_im)

    r, c, src = _time_major_perm()
    to_time_major = jnp.where(c == src, 1.0, 0.0).astype(BF16)
    to_batch_major = jnp.where(r == (c % BATCH) * S5_PERM_STEPS + c // BATCH, 1.0, 0.0).astype(BF16)

    g = g_ref[layer:layer + 1, :]
    u_parts = []
    for part in range(S5_STEPS // S5_PERM_STEPS):
        steps = slice(part * S5_PERM_STEPS, (part + 1) * S5_PERM_STEPS)
        h_rows = []
        for b in range(BATCH):
            h_b = _rmsnorm(x_ref[b, steps, :], g) * (1.0 + mod_ref[SCALE1, b:b + 1, :])
            h_rows.append((h_b + mod_ref[SHIFT1, b:b + 1, :]).astype(BF16))
        h_part = jnp.concatenate(h_rows, axis=0)
        h_tm = _dot(to_time_major, h_part).astype(BF16)
        u_parts.append(_dot(h_tm, win_ref[...]))
    u = jnp.concatenate(u_parts, axis=0)
    ub = u.astype(BF16)
    block_ids = [(k, c0) for k in range(S5_NSUPER) for c0 in range(0, S5_SUPER_STATE, SCAN_COLS)]
    y_parts = [None] * S5_NSUPER
    for first in range(0, len(block_ids), SCAN_INFLIGHT):
        blocks = []
        for k, c0 in block_ids[first:first + SCAN_INFLIGHT]:
            uk = ub[:, k * MXU_DIM:(k + 1) * MXU_DIM]
            cols = slice(c0, c0 + SCAN_COLS)
            gcols = slice(k * S5_SUPER_STATE + c0, k * S5_SUPER_STATE + c0 + SCAN_COLS)
            bre_ref[:, cols] = _dot(uk, wbre_ref[k, :, cols])
            bim_ref[:, cols] = _dot(uk, wbim_ref[k, :, cols])
            blocks.append(dict(
                k=k, cols=cols, gcols=gcols,
                a_re=jnp.broadcast_to(are_ref[:, gcols], (BATCH, SCAN_COLS)),
                a_im=jnp.broadcast_to(aim_ref[:, gcols], (BATCH, SCAN_COLS)),
                s_re=state_re[:, gcols], s_im=state_im[:, gcols]))
        for t in range(S5_STEPS):
            rows = slice(t * BATCH, (t + 1) * BATCH)
            for blk in blocks:
                s_re, s_im, cols = blk["s_re"], blk["s_im"], blk["cols"]
                blk["s_re"] = blk["a_re"] * s_re - blk["a_im"] * s_im + bre_ref[rows, cols]
                blk["s_im"] = blk["a_re"] * s_im + blk["a_im"] * s_re + bim_ref[rows, cols]
                bre_ref[rows, cols] = blk["s_re"]
                bim_ref[rows, cols] = blk["s_im"]
        for blk in blocks:
            k, cols = blk["k"], blk["cols"]
            state_re[:, blk["gcols"]] = blk["s_re"]
            state_im[:, blk["gcols"]] = blk["s_im"]
            part = (_dot(bre_ref[:, cols].astype(BF16), wcre_ref[k, cols, :])
                    - _dot(bim_ref[:, cols].astype(BF16), wcim_ref[k, cols, :]))
            y_parts[k] = part if y_parts[k] is None else y_parts[k] + part
    for k in range(S5_NSUPER):
        y_ref[:, k * MXU_DIM:(k + 1) * MXU_DIM] = y_parts[k]

    y = jax.nn.gelu(y_ref[...] + dskip_ref[idx:idx + 1, :] * u)
    y = y * jax.nn.sigmoid(_dot(y.astype(BF16), gluw_ref[...]) + glub_ref[idx:idx + 1, :])
    yb = y.astype(BF16)

    for part in range(S5_STEPS // S5_PERM_STEPS):
        steps = slice(part * S5_PERM_STEPS, (part + 1) * S5_PERM_STEPS)
        y_part = _dot(to_batch_major, yb[part * MXU_DIM:(part + 1) * MXU_DIM, :]).astype(BF16)
        out = _dot(y_part, wout_ref[...])
        for b in range(BATCH):
            out_b = out[b * S5_PERM_STEPS:(b + 1) * S5_PERM_STEPS, :]
            o_ref[b, steps, :] = x_ref[b, steps, :] + mod_ref[GATE1, b:b + 1, :] * out_b


def _s5_layer(x, mods, norm_g, w_in, glu_w, w_out, abar_re, abar_im, wb_re, wb_im, wc_re, wc_im,
              d_skip, glu_b, *, casts, layer, idx):
    d = D_MODEL
    n = d_skip.shape[0]
    n_steps = SEQ // S5_STEPS
    x_spec = pl.BlockSpec((BATCH, S5_STEPS, d), lambda i: (0, i, 0))
    c_in, c_out, c_shapes, c_args = _cast_plan(casts, n_steps, lambda i: i)
    out, *cast_out = pl.pallas_call(
        functools.partial(_s5_kernel, layer=layer, idx=idx, n_cast=len(casts)),
        grid=(n_steps,),
        in_specs=[x_spec, _mod_spec(layer), _const_spec((DEPTH, d)),
                  _const_spec((d, D_MIX)),
                  _const_spec((1, S5_NSTATE)), _const_spec((1, S5_NSTATE)),
                  _const_spec((S5_NSUPER, MXU_DIM, S5_SUPER_STATE)),
                  _const_spec((S5_NSUPER, MXU_DIM, S5_SUPER_STATE)),
                  _const_spec((S5_NSUPER, S5_SUPER_STATE, MXU_DIM)),
                  _const_spec((S5_NSUPER, S5_SUPER_STATE, MXU_DIM)),
                  _const_spec((n, D_MIX)), _const_spec((D_MIX, D_MIX)),
                  _const_spec((n, D_MIX)), _const_spec((D_MIX, d))] + c_in,
        out_specs=[x_spec] + c_out,
        out_shape=[jax.ShapeDtypeStruct((BATCH, SEQ, d), F32)] + c_shapes,
        scratch_shapes=[pltpu.VMEM((S5_ROWS, S5_SUPER_STATE), F32),
                        pltpu.VMEM((S5_ROWS, S5_SUPER_STATE), F32),
                        pltpu.VMEM((BATCH, S5_NSTATE), F32),
                        pltpu.VMEM((BATCH, S5_NSTATE), F32),
                        pltpu.VMEM((S5_ROWS, D_MIX), F32)],
        compiler_params=_params("arbitrary"),
        name="s5_mixer",
    )(x, mods, norm_g, w_in, abar_re, abar_im, wb_re, wb_im, wc_re, wc_im, d_skip,
      glu_w, glu_b, w_out, *c_args)
    return out, cast_out


def kernel(x, c, ada_w, ada_b, norm1_g, norm2_g, ff_w1, ff_w2, final_g, conv_w_in, conv_w, conv_b, conv_w_out, ssm_w_in, ssm_a_re, ssm_a_im, ssm_log_dt, ssm_b_re, ssm_b_im, ssm_c_re, ssm_c_im, ssm_d, ssm_glu_w, ssm_glu_b, ssm_w_out, sg_w_in, sg_v_g, sg_w_s, sg_b_s, sg_w_out):
    mods = _adaln(c, ada_w, ada_b)
    sg_b_s_t = sg_b_s.transpose(0, 2, 1)
    final_g = final_g.reshape(1, D_MODEL)

    def mixer_weights(i):
        j = i // N_MIXERS
        stacks = [(conv_w_in, conv_w_out), (ssm_w_in, ssm_glu_w, ssm_w_out), (sg_w_in, sg_w_out)]
        return [(w, j) for w in stacks[i % N_MIXERS]]

    w_mix = [w[j].astype(BF16) for w, j in mixer_weights(0)]
    for i in range(DEPTH):
        kind = i % N_MIXERS
        j = i // N_MIXERS
        casts = [(ff_w1, i), (ff_w2, i)] + (mixer_weights(i + 1) if i + 1 < DEPTH else [])
        if kind == 0:
            x, cast_out = _conv_layer(x, mods, norm1_g, *w_mix, conv_w, conv_b,
                                      casts=casts, layer=i, idx=j)
        elif kind == 1:
            abar_re, abar_im, wb_re, wb_im, wc_re, wc_im = _s5_prep(
                ssm_a_re[j], ssm_a_im[j], ssm_log_dt[j], ssm_b_re[j], ssm_b_im[j],
                ssm_c_re[j], ssm_c_im[j])
            x, cast_out = _s5_layer(x, mods, norm1_g, *w_mix,
                                    abar_re.reshape(1, S5_NSTATE), abar_im.reshape(1, S5_NSTATE),
                                    wb_re, wb_im, wc_re, wc_im, ssm_d, ssm_glu_b,
                                    casts=casts, layer=i, idx=j)
        else:
            x, cast_out = _sg_layer(x, mods, norm1_g, *w_mix, sg_v_g, sg_w_s, sg_b_s_t,
                                    casts=casts, layer=i, idx=j)
        w1, w2, *w_mix = cast_out
        x = _mlp_layer(x, mods, norm2_g, w1, w2, final_g, layer=i)
    return x
```

```python
import functools

import jax
import jax.numpy as jnp
from jax import lax
from jax.experimental import pallas as pl
from jax.experimental.pallas import tpu as pltpu

D_MODEL = 1024
BATCH = 8
SEQ = 4096
DEPTH = 4
N_MIXERS = 3
D_FF = 4 * D_MODEL
D_MIX = D_MODEL
EPS = 1e-6
CONV_WIDTH = 3
S5_GROUP = 16
S5_GROUPS = D_MIX // S5_GROUP
S5_STATE = 64
CHUNK = 128
SG_HEADS = 8
SG_HEAD_DIM = D_MIX // SG_HEADS

SUBLANES = 8
MXU_DIM = 256

S5_SUPER = MXU_DIM // S5_GROUP
S5_NSUPER = S5_GROUPS // S5_SUPER
S5_SUPER_STATE = S5_SUPER * S5_STATE
S5_NSTATE = S5_GROUPS * S5_STATE

ROW_TILE = 1024
S5_ROWS = 1024
S5_STEPS = S5_ROWS // BATCH
S5_PERM_STEPS = MXU_DIM // BATCH
FF_TILE = 1024
SCAN_COLS = 256
SCAN_INFLIGHT = 2
VMEM_LIMIT = 60 * 1024 * 1024

N_MOD = 6
SHIFT1, SCALE1, GATE1, SHIFT2, SCALE2, GATE2 = range(N_MOD)

F32 = jnp.float32
BF16 = jnp.bfloat16


def _dot(a, b):
    return jnp.dot(a, b, preferred_element_type=F32)


def _rmsnorm(x, g):
    y = x * lax.rsqrt(jnp.mean(x * x, axis=-1, keepdims=True) + EPS)
    return y * g


def _const_spec(shape):
    zeros = (0,) * len(shape)
    return pl.BlockSpec(shape, lambda *_: zeros, pipeline_mode=pl.Buffered(1))


def _layer_spec(shape, layer):
    zeros = (0,) * len(shape)
    return pl.BlockSpec((None,) + tuple(shape), lambda *_: (layer,) + zeros,
                        pipeline_mode=pl.Buffered(1))


def _params(*semantics):
    return pltpu.CompilerParams(dimension_semantics=semantics, vmem_limit_bytes=VMEM_LIMIT)


BF16_SUBLANES = 16


def _cast_plan(casts, n_steps, step_index):
    in_specs, out_specs, out_shapes, args = [], [], [], []
    for stacked, entry in casts:
        _, rows, cols = stacked.shape
        block_rows, rem = divmod(rows, n_steps)
        assert rem == 0 and block_rows % BF16_SUBLANES == 0, (rows, n_steps)
        in_specs.append(pl.BlockSpec((None, block_rows, cols),
                                     lambda *g, entry=entry: (entry, step_index(*g), 0)))
        out_specs.append(pl.BlockSpec((block_rows, cols), lambda *g: (step_index(*g), 0)))
        out_shapes.append(jax.ShapeDtypeStruct((rows, cols), BF16))
        args.append(stacked)
    return in_specs, out_specs, out_shapes, args


def _split_cast_refs(rest, n_cast):
    cast_in = rest[:n_cast]
    o_ref = rest[n_cast]
    cast_out = rest[n_cast + 1:2 * n_cast + 1]
    return list(zip(cast_in, cast_out)), o_ref, rest[2 * n_cast + 1:]


def _cast_blocks(casts):
    for src, dst in casts:
        dst[...] = src[...].astype(BF16)


def _tile_step(b, l):
    return b * (SEQ // ROW_TILE) + l


ADALN_ROWS = 3


def _adaln_kernel(c_ref, w_ref, b_ref, o_ref):
    d = D_MODEL
    c = c_ref[...]
    c_act = (c * jax.nn.sigmoid(c)).astype(BF16)
    mod = _dot(c_act, w_ref[...].astype(BF16)) + b_ref[...]
    for j in range(ADALN_ROWS):
        o_ref[j] = mod[:, j * d:(j + 1) * d]


def _adaln(c, ada_w, ada_b):
    d = D_MODEL
    return pl.pallas_call(
        _adaln_kernel,
        grid=(DEPTH, N_MOD // ADALN_ROWS),
        in_specs=[
            _const_spec((BATCH, d)),
            pl.BlockSpec((None, d, ADALN_ROWS * d), lambda i, j: (i, 0, j)),
            pl.BlockSpec((None, 1, ADALN_ROWS * d), lambda i, j: (i, 0, j)),
        ],
        out_specs=pl.BlockSpec((None, ADALN_ROWS, BATCH, d), lambda i, j: (i, j, 0, 0)),
        out_shape=jax.ShapeDtypeStruct((DEPTH, N_MOD, BATCH, d), F32),
        compiler_params=_params("arbitrary", "arbitrary"),
        name="adaln",
    )(c, ada_w, ada_b.reshape(DEPTH, 1, N_MOD * d))


def _row_spec():
    return pl.BlockSpec((None, ROW_TILE, D_MODEL), lambda b, l: (b, l, 0))


def _mod_spec(layer):
    return _layer_spec((N_MOD, BATCH, D_MODEL), layer)


def _mod_row(mod_ref, which):
    return mod_ref[which, pl.ds(pl.program_id(0), 1), :]


def _mlp_kernel(x_ref, mod_ref, g_ref, w1_ref, w2_ref, fg_ref, o_ref, *, layer, final):
    x = x_ref[...]
    h = _rmsnorm(x, g_ref[layer:layer + 1, :]) * (1.0 + _mod_row(mod_ref, SCALE2))
    h = (h + _mod_row(mod_ref, SHIFT2)).astype(BF16)
    acc = jnp.zeros((ROW_TILE, D_MODEL), F32)
    for k in range(D_FF // FF_TILE):
        hid = _dot(h, w1_ref[:, k * FF_TILE:(k + 1) * FF_TILE])
        hid = jnp.square(jnp.maximum(hid, 0.0)).astype(BF16)
        acc = acc + _dot(hid, w2_ref[k * FF_TILE:(k + 1) * FF_TILE, :])
    out = x + _mod_row(mod_ref, GATE2) * acc
    if final:
        out = _rmsnorm(out, fg_ref[...])
    o_ref[...] = out


def _mlp_layer(x, mods, norm_g, w1, w2, final_g, *, layer):
    d = D_MODEL
    return pl.pallas_call(
        functools.partial(_mlp_kernel, layer=layer, final=layer == DEPTH - 1),
        grid=(BATCH, SEQ // ROW_TILE),
        in_specs=[_row_spec(), _mod_spec(layer), _const_spec((DEPTH, d)),
                  _const_spec((d, D_FF)), _const_spec((D_FF, d)), _const_spec((1, d))],
        out_specs=_row_spec(),
        out_shape=jax.ShapeDtypeStruct((BATCH, SEQ, d), F32),
        compiler_params=_params("parallel", "arbitrary"),
        name="mlp",
    )(x, mods, norm_g, w1, w2, final_g)


def _conv_kernel(x_ref, mod_ref, g_ref, win_ref, cw_ref, cb_ref, wout_ref, *rest,
                 layer, idx, n_cast):
    casts, o_ref, (zbuf,) = _split_cast_refs(rest, n_cast)
    _cast_blocks(casts)
    d = D_MIX
    halo = SUBLANES

    @pl.when(pl.program_id(1) == 0)
    def _():
        zbuf[0:halo, :] = jnp.zeros((halo, d), F32)

    x = x_ref[...]
    h = _rmsnorm(x, g_ref[layer:layer + 1, :]) * (1.0 + _mod_row(mod_ref, SCALE1))
    h = (h + _mod_row(mod_ref, SHIFT1)).astype(BF16)
    bcx = _dot(h, win_ref[...])
    b_gate = bcx[:, 0:d]
    z = bcx[:, d:2 * d] * bcx[:, 2 * d:3 * d]
    zbuf[halo:halo + ROW_TILE, :] = z
    conv = (cw_ref[idx, 2:3, :] * z
            + cw_ref[idx, 1:2, :] * zbuf[halo - 1:halo - 1 + ROW_TILE, :]
            + cw_ref[idx, 0:1, :] * zbuf[halo - 2:halo - 2 + ROW_TILE, :]
            + cb_ref[idx:idx + 1, :])
    zbuf[0:halo, :] = zbuf[ROW_TILE:ROW_TILE + halo, :]
    y = _dot((b_gate * conv).astype(BF16), wout_ref[...])
    o_ref[...] = x + _mod_row(mod_ref, GATE1) * y


def _conv_layer(x, mods, norm_g, w_in, w_out, cw, cb, *, casts, layer, idx):
    d = D_MODEL
    n = cw.shape[0]
    grid = (BATCH, SEQ // ROW_TILE)
    c_in, c_out, c_shapes, c_args = _cast_plan(casts, grid[0] * grid[1], _tile_step)
    out, *cast_out = pl.pallas_call(
        functools.partial(_conv_kernel, layer=layer, idx=idx, n_cast=len(casts)),
        grid=grid,
        in_specs=[_row_spec(), _mod_spec(layer), _const_spec((DEPTH, d)),
                  _const_spec((d, 3 * D_MIX)), _const_spec((n, CONV_WIDTH, D_MIX)),
                  _const_spec((n, D_MIX)), _const_spec((D_MIX, d))] + c_in,
        out_specs=[_row_spec()] + c_out,
        out_shape=[jax.ShapeDtypeStruct((BATCH, SEQ, d), F32)] + c_shapes,
        scratch_shapes=[pltpu.VMEM((ROW_TILE + SUBLANES, D_MIX), F32)],
        compiler_params=_params("arbitrary", "arbitrary"),
        name="conv_mixer",
    )(x, mods, norm_g, w_in, cw, cb, w_out, *c_args)
    return out, cast_out


def _sg_kernel(x_ref, mod_ref, g_ref, win_ref, vg_ref, ws_ref, bs_ref, wout_ref, *rest,
               layer, idx, n_cast):
    casts, o_ref, (vm_ref,) = _split_cast_refs(rest, n_cast)
    _cast_blocks(casts)
    d = D_MIX
    x = x_ref[...]
    h = _rmsnorm(x, g_ref[layer:layer + 1, :]) * (1.0 + _mod_row(mod_ref, SCALE1))
    h = (h + _mod_row(mod_ref, SHIFT1)).astype(BF16)
    uv = _dot(h, win_ref[...])
    u = uv[:, 0:d]
    v = _rmsnorm(uv[:, d:2 * d], vg_ref[idx:idx + 1, :]).astype(BF16)
    row = lax.broadcasted_iota(jnp.int32, (CHUNK, CHUNK), 0)
    col = lax.broadcasted_iota(jnp.int32, (CHUNK, CHUNK), 1)
    causal = col <= row
    for hd in range(SG_HEADS):
        ws = jnp.where(causal, ws_ref[hd], 0.0).astype(BF16)
        bias = bs_ref[:, hd:hd + 1]
        lanes = slice(hd * SG_HEAD_DIM, (hd + 1) * SG_HEAD_DIM)
        for n in range(ROW_TILE // CHUNK):
            rows = slice(n * CHUNK, (n + 1) * CHUNK)
            vm_ref[rows, lanes] = _dot(ws, v[rows, lanes]) + bias
    y = _dot((u * vm_ref[...]).astype(BF16), wout_ref[...])
    o_ref[...] = x + _mod_row(mod_ref, GATE1) * y


def _sg_layer(x, mods, norm_g, w_in, w_out, v_g, w_s, b_s_t, *, casts, layer, idx):
    d = D_MODEL
    n = v_g.shape[0]
    grid = (BATCH, SEQ // ROW_TILE)
    c_in, c_out, c_shapes, c_args = _cast_plan(casts, grid[0] * grid[1], _tile_step)
    out, *cast_out = pl.pallas_call(
        functools.partial(_sg_kernel, layer=layer, idx=idx, n_cast=len(casts)),
        grid=grid,
        in_specs=[_row_spec(), _mod_spec(layer), _const_spec((DEPTH, d)),
                  _const_spec((d, 2 * D_MIX)), _const_spec((n, D_MIX)),
                  _layer_spec((SG_HEADS, CHUNK, CHUNK), idx), _layer_spec((CHUNK, SG_HEADS), idx),
                  _const_spec((D_MIX, d))] + c_in,
        out_specs=[_row_spec()] + c_out,
        out_shape=[jax.ShapeDtypeStruct((BATCH, SEQ, d), F32)] + c_shapes,
        scratch_shapes=[pltpu.VMEM((ROW_TILE, D_MIX), F32)],
        compiler_params=_params("arbitrary", "arbitrary"),
        name="sg_mixer",
    )(x, mods, norm_g, w_in, v_g, w_s, b_s_t, w_out, *c_args)
    return out, cast_out


def _replicate_block_diag(w, n_rep, row_block, col_block):
    rows = w.shape[0]
    src = lax.broadcasted_iota(jnp.int32, (col_block, n_rep * col_block), 0)
    dst = lax.broadcasted_iota(jnp.int32, (col_block, n_rep * col_block), 1)
    replicate = jnp.where(dst % col_block == src, 1.0, 0.0).astype(BF16)
    tiled = _dot(w.astype(BF16), replicate)
    r = lax.broadcasted_iota(jnp.int32, (rows, n_rep * col_block), 0)
    c = lax.broadcasted_iota(jnp.int32, (rows, n_rep * col_block), 1)
    return jnp.where(r // row_block == c // col_block, tiled, 0.0).astype(BF16)


def _s5_prep_kernel(are_ref, aim_ref, ldt_ref, bre_ref, bim_ref, cre_ref, cim_ref,
                    abar_re_ref, abar_im_ref, wbre_ref, wbim_ref, wcre_ref, wcim_ref):
    a_re = are_ref[...]
    a_im = aim_ref[...]
    dt = jnp.exp(ldt_ref[...])
    mag = jnp.exp(a_re * dt)
    abar_re = mag * jnp.cos(a_im * dt)
    abar_im = mag * jnp.sin(a_im * dt)
    den = a_re * a_re + a_im * a_im
    nr = abar_re - 1.0
    ni = abar_im
    f_re = ((nr * a_re + ni * a_im) / den)[:, None, :]
    f_im = ((ni * a_re - nr * a_im) / den)[:, None, :]
    b_re = bre_ref[...]
    b_im = bim_ref[...]
    abar_re_ref[...] = abar_re
    abar_im_ref[...] = abar_im
    bbar_re = (f_re * b_re - f_im * b_im).reshape(S5_GROUPS * S5_GROUP, S5_STATE)
    bbar_im = (f_re * b_im + f_im * b_re).reshape(S5_GROUPS * S5_GROUP, S5_STATE)
    for k in range(S5_NSUPER):
        rows_in = slice(k * MXU_DIM, (k + 1) * MXU_DIM)
        wbre_ref[k] = _replicate_block_diag(bbar_re[rows_in, :], S5_SUPER, S5_GROUP, S5_STATE)
        wbim_ref[k] = _replicate_block_diag(bbar_im[rows_in, :], S5_SUPER, S5_GROUP, S5_STATE)
        rows_out = slice(k * S5_SUPER_STATE, (k + 1) * S5_SUPER_STATE)
        wcre_ref[k] = _replicate_block_diag(cre_ref[rows_out, :], S5_SUPER, S5_STATE, S5_GROUP)
        wcim_ref[k] = _replicate_block_diag(cim_ref[rows_out, :], S5_SUPER, S5_STATE, S5_GROUP)


def _s5_prep(a_re, a_im, log_dt, b_re, b_im, c_re, c_im):
    g, p, h = S5_GROUPS, S5_STATE, S5_GROUP
    state_major = lambda c: c.transpose(0, 2, 1).reshape(g * p, h)
    return pl.pallas_call(
        _s5_prep_kernel,
        out_shape=[jax.ShapeDtypeStruct((g, p), F32), jax.ShapeDtypeStruct((g, p), F32),
                   jax.ShapeDtypeStruct((S5_NSUPER, MXU_DIM, S5_SUPER_STATE), BF16),
                   jax.ShapeDtypeStruct((S5_NSUPER, MXU_DIM, S5_SUPER_STATE), BF16),
                   jax.ShapeDtypeStruct((S5_NSUPER, S5_SUPER_STATE, MXU_DIM), BF16),
                   jax.ShapeDtypeStruct((S5_NSUPER, S5_SUPER_STATE, MXU_DIM), BF16)],
        name="s5_prep",
    )(a_re, a_im, log_dt.reshape(g, 1), b_re.transpose(0, 2, 1), b_im.transpose(0, 2, 1),
      state_major(c_re), state_major(c_im))


def _s5_kernel(x_ref, mod_ref, g_ref, win_ref, are_ref, aim_ref, wbre_ref, wbim_ref,
               wcre_ref, wcim_ref, dskip_ref, gluw_ref, glub_ref, wout_ref, *rest,
               layer, idx, n_cast):
    casts, o_ref, scratch = _split_cast_refs(rest, n_cast)
    bre_ref, bim_ref, state_re, state_im, y_ref = scratch
    _cast_blocks(casts)
    d = D_MODEL

    @pl.when(pl.program_id(0) == 0)
    def _():
        state_re[...] = jnp.zeros_like(state_re)
        state_im[...] = jnp.zeros_like(state_im)

    g = g_ref[layer:layer + 1, :]
    u_parts = []
    for part in range(S5_STEPS // S5_PERM_STEPS):
        steps = slice(part * S5_PERM_STEPS, (part + 1) * S5_PERM_STEPS)
        h_rows = []
        for b in range(BATCH):
            h_b = _rmsnorm(x_ref[b, steps, :], g) * (1.0 + mod_ref[SCALE1, b:b + 1, :])
            h_rows.append((h_b + mod_ref[SHIFT1, b:b + 1, :]).astype(BF16))
        h_part = jnp.concatenate(h_rows, axis=0)
        u_bm = _dot(h_part, win_ref[...]).reshape(BATCH, S5_PERM_STEPS, d)
        u_parts.append(jnp.swapaxes(u_bm, 0, 1).reshape(MXU_DIM, d))
    u = jnp.concatenate(u_parts, axis=0)
    ub = u.astype(BF16)
    block_ids = [(k, c0) for k in range(S5_NSUPER) for c0 in range(0, S5_SUPER_STATE, SCAN_COLS)]
    y_parts = [None] * S5_NSUPER
    for first in range(0, len(block_ids), SCAN_INFLIGHT):
        blocks = []
        for k, c0 in block_ids[first:first + SCAN_INFLIGHT]:
            uk = ub[:, k * MXU_DIM:(k + 1) * MXU_DIM]
            cols = slice(c0, c0 + SCAN_COLS)
            gcols = slice(k * S5_SUPER_STATE + c0, k * S5_SUPER_STATE + c0 + SCAN_COLS)
            bre_ref[:, cols] = _dot(uk, wbre_ref[k, :, cols])
            bim_ref[:, cols] = _dot(uk, wbim_ref[k, :, cols])
            blocks.append(dict(
                k=k, cols=cols, gcols=gcols,
                a_re=jnp.broadcast_to(are_ref[:, gcols], (BATCH, SCAN_COLS)),
                a_im=jnp.broadcast_to(aim_ref[:, gcols], (BATCH, SCAN_COLS)),
                s_re=state_re[:, gcols], s_im=state_im[:, gcols]))
        for t in range(S5_STEPS):
            rows = slice(t * BATCH, (t + 1) * BATCH)
            for blk in blocks:
                s_re, s_im, cols = blk["s_re"], blk["s_im"], blk["cols"]
                blk["s_re"] = blk["a_re"] * s_re - blk["a_im"] * s_im + bre_ref[rows, cols]
                blk["s_im"] = blk["a_re"] * s_im + blk["a_im"] * s_re + bim_ref[rows, cols]
                bre_ref[rows, cols] = blk["s_re"]
                bim_ref[rows, cols] = blk["s_im"]
        for blk in blocks:
            k, cols = blk["k"], blk["cols"]
            state_re[:, blk["gcols"]] = blk["s_re"]
            state_im[:, blk["gcols"]] = blk["s_im"]
            part = (_dot(bre_ref[:, cols].astype(BF16), wcre_ref[k, cols, :])
                    - _dot(bim_ref[:, cols].astype(BF16), wcim_ref[k, cols, :]))
            y_parts[k] = part if y_parts[k] is None else y_parts[k] + part
    for k in range(S5_NSUPER):
        y_ref[:, k * MXU_DIM:(k + 1) * MXU_DIM] = y_parts[k]

    y = jax.nn.gelu(y_ref[...] + dskip_ref[idx:idx + 1, :] * u)
    y = y * jax.nn.sigmoid(_dot(y.astype(BF16), gluw_ref[...]) + glub_ref[idx:idx + 1, :])

    for part in range(S5_STEPS // S5_PERM_STEPS):
        steps = slice(part * S5_PERM_STEPS, (part + 1) * S5_PERM_STEPS)
        y_tm = y[part * MXU_DIM:(part + 1) * MXU_DIM, :].reshape(S5_PERM_STEPS, BATCH, d)
        y_part = jnp.swapaxes(y_tm, 0, 1).reshape(MXU_DIM, d).astype(BF16)
        out = _dot(y_part, wout_ref[...])
        for b in range(BATCH):
            out_b = out[b * S5_PERM_STEPS:(b + 1) * S5_PERM_STEPS, :]
            o_ref[b, steps, :] = x_ref[b, steps, :] + mod_ref[GATE1, b:b + 1, :] * out_b


def _s5_layer(x, mods, norm_g, w_in, glu_w, w_out, abar_re, abar_im, wb_re, wb_im, wc_re, wc_im,
              d_skip, glu_b, *, casts, layer, idx):
    d = D_MODEL
    n = d_skip.shape[0]
    n_steps = SEQ // S5_STEPS
    x_spec = pl.BlockSpec((BATCH, S5_STEPS, d), lambda i: (0, i, 0))
    c_in, c_out, c_shapes, c_args = _cast_plan(casts, n_steps, lambda i: i)
    out, *cast_out = pl.pallas_call(
        functools.partial(_s5_kernel, layer=layer, idx=idx, n_cast=len(casts)),
        grid=(n_steps,),
        in_specs=[x_spec, _mod_spec(layer), _const_spec((DEPTH, d)),
                  _const_spec((d, D_MIX)),
                  _const_spec((1, S5_NSTATE)), _const_spec((1, S5_NSTATE)),
                  _const_spec((S5_NSUPER, MXU_DIM, S5_SUPER_STATE)),
                  _const_spec((S5_NSUPER, MXU_DIM, S5_SUPER_STATE)),
                  _const_spec((S5_NSUPER, S5_SUPER_STATE, MXU_DIM)),
                  _const_spec((S5_NSUPER, S5_SUPER_STATE, MXU_DIM)),
                  _const_spec((n, D_MIX)), _const_spec((D_MIX, D_MIX)),
                  _const_spec((n, D_MIX)), _const_spec((D_MIX, d))] + c_in,
        out_specs=[x_spec] + c_out,
        out_shape=[jax.ShapeDtypeStruct((BATCH, SEQ, d), F32)] + c_shapes,
        scratch_shapes=[pltpu.VMEM((S5_ROWS, S5_SUPER_STATE), F32),
                        pltpu.VMEM((S5_ROWS, S5_SUPER_STATE), F32),
                        pltpu.VMEM((BATCH, S5_NSTATE), F32),
                        pltpu.VMEM((BATCH, S5_NSTATE), F32),
                        pltpu.VMEM((S5_ROWS, D_MIX), F32)],
        compiler_params=_params("arbitrary"),
        name="s5_mixer",
    )(x, mods, norm_g, w_in, abar_re, abar_im, wb_re, wb_im, wc_re, wc_im, d_skip,
      glu_w, glu_b, w_out, *c_args)
    return out, cast_out


def kernel(x, c, ada_w, ada_b, norm1_g, norm2_g, ff_w1, ff_w2, final_g, conv_w_in, conv_w, conv_b, conv_w_out, ssm_w_in, ssm_a_re, ssm_a_im, ssm_log_dt, ssm_b_re, ssm_b_im, ssm_c_re, ssm_c_im, ssm_d, ssm_glu_w, ssm_glu_b, ssm_w_out, sg_w_in, sg_v_g, sg_w_s, sg_b_s, sg_w_out):
    mods = _adaln(c, ada_w, ada_b)
    sg_b_s_t = sg_b_s.transpose(0, 2, 1)
    final_g = final_g.reshape(1, D_MODEL)

    def mixer_weights(i):
        j = i // N_MIXERS
        stacks = [(conv_w_in, conv_w_out), (ssm_w_in, ssm_glu_w, ssm_w_out), (sg_w_in, sg_w_out)]
        return [(w, j) for w in stacks[i % N_MIXERS]]

    w_mix = [w[j].astype(BF16) for w, j in mixer_weights(0)]
    for i in range(DEPTH):
        kind = i % N_MIXERS
        j = i // N_MIXERS
        casts = [(ff_w1, i), (ff_w2, i)] + (mixer_weights(i + 1) if i + 1 < DEPTH else [])
        if kind == 0:
            x, cast_out = _conv_layer(x, mods, norm1_g, *w_mix, conv_w, conv_b,
                                      casts=casts, layer=i, idx=j)
        elif kind == 1:
            abar_re, abar_im, wb_re, wb_im, wc_re, wc_im = _s5_prep(
                ssm_a_re[j], ssm_a_im[j], ssm_log_dt[j], ssm_b_re[j], ssm_b_im[j],
                ssm_c_re[j], ssm_c_im[j])
            x, cast_out = _s5_layer(x, mods, norm1_g, *w_mix,
                                    abar_re.reshape(1, S5_NSTATE), abar_im.reshape(1, S5_NSTATE),
                                    wb_re, wb_im, wc_re, wc_im, ssm_d, ssm_glu_b,
                                    casts=casts, layer=i, idx=j)
        else:
            x, cast_out = _sg_layer(x, mods, norm1_g, *w_mix, sg_v_g, sg_w_s, sg_b_s_t,
                                    casts=casts, layer=i, idx=j)
        w1, w2, *w_mix = cast_out
        x = _mlp_layer(x, mods, norm2_g, w1, w2, final_g, layer=i)
    return x
```

```python
import functools

import jax
import jax.numpy as jnp
from jax import lax
from jax.experimental import pallas as pl
from jax.experimental.pallas import tpu as pltpu

D_MODEL = 1024
BATCH = 8
SEQ = 4096
DEPTH = 4
N_MIXERS = 3
D_FF = 4 * D_MODEL
D_MIX = D_MODEL
EPS = 1e-6
CONV_WIDTH = 3
S5_GROUP = 16
S5_GROUPS = D_MIX // S5_GROUP
S5_STATE = 64
CHUNK = 128
SG_HEADS = 8
SG_HEAD_DIM = D_MIX // SG_HEADS

SUBLANES = 8
MXU_DIM = 256

S5_SUPER = MXU_DIM // S5_GROUP
S5_NSUPER = S5_GROUPS // S5_SUPER
S5_SUPER_STATE = S5_SUPER * S5_STATE
S5_NSTATE = S5_GROUPS * S5_STATE

ROW_TILE = 1024
S5_ROWS = 1024
S5_STEPS = S5_ROWS // BATCH
S5_PERM_STEPS = MXU_DIM // BATCH
FF_TILE = 1024
SCAN_COLS = 256
SCAN_INFLIGHT = 2
assert SCAN_INFLIGHT <= S5_SUPER_STATE // SCAN_COLS
VMEM_LIMIT = 60 * 1024 * 1024

N_MOD = 6
SHIFT1, SCALE1, GATE1, SHIFT2, SCALE2, GATE2 = range(N_MOD)

F32 = jnp.float32
BF16 = jnp.bfloat16


def _dot(a, b):
    return jnp.dot(a, b, preferred_element_type=F32)


def _rmsnorm(x, g):
    y = x * lax.rsqrt(jnp.mean(x * x, axis=-1, keepdims=True) + EPS)
    return y * g


def _const_spec(shape):
    zeros = (0,) * len(shape)
    return pl.BlockSpec(shape, lambda *_: zeros, pipeline_mode=pl.Buffered(1))


def _layer_spec(shape, layer):
    zeros = (0,) * len(shape)
    return pl.BlockSpec((None,) + tuple(shape), lambda *_: (layer,) + zeros,
                        pipeline_mode=pl.Buffered(1))


def _params(*semantics):
    return pltpu.CompilerParams(dimension_semantics=semantics, vmem_limit_bytes=VMEM_LIMIT)


BF16_SUBLANES = 16


def _cast_plan(casts, n_steps, step_index):
    in_specs, out_specs, out_shapes, args = [], [], [], []
    for stacked, entry in casts:
        _, rows, cols = stacked.shape
        block_rows, rem = divmod(rows, n_steps)
        assert rem == 0 and block_rows % BF16_SUBLANES == 0, (rows, n_steps)
        in_specs.append(pl.BlockSpec((None, block_rows, cols),
                                     lambda *g, entry=entry: (entry, step_index(*g), 0)))
        out_specs.append(pl.BlockSpec((block_rows, cols), lambda *g: (step_index(*g), 0)))
        out_shapes.append(jax.ShapeDtypeStruct((rows, cols), BF16))
        args.append(stacked)
    return in_specs, out_specs, out_shapes, args


def _split_cast_refs(rest, n_cast):
    cast_in = rest[:n_cast]
    o_ref = rest[n_cast]
    cast_out = rest[n_cast + 1:2 * n_cast + 1]
    return list(zip(cast_in, cast_out)), o_ref, rest[2 * n_cast + 1:]


def _cast_blocks(casts):
    for src, dst in casts:
        dst[...] = src[...].astype(BF16)


def _tile_step(b, l):
    return b * (SEQ // ROW_TILE) + l


ADALN_ROWS = 3


def _adaln_kernel(c_ref, w_ref, b_ref, o_ref):
    d = D_MODEL
    c = c_ref[...]
    c_act = (c * jax.nn.sigmoid(c)).astype(BF16)
    mod = _dot(c_act, w_ref[...].astype(BF16)) + b_ref[...]
    for j in range(ADALN_ROWS):
        o_ref[j] = mod[:, j * d:(j + 1) * d]


def _adaln(c, ada_w, ada_b):
    d = D_MODEL
    return pl.pallas_call(
        _adaln_kernel,
        grid=(DEPTH, N_MOD // ADALN_ROWS),
        in_specs=[
            _const_spec((BATCH, d)),
            pl.BlockSpec((None, d, ADALN_ROWS * d), lambda i, j: (i, 0, j)),
            pl.BlockSpec((None, 1, ADALN_ROWS * d), lambda i, j: (i, 0, j)),
        ],
        out_specs=pl.BlockSpec((None, ADALN_ROWS, BATCH, d), lambda i, j: (i, j, 0, 0)),
        out_shape=jax.ShapeDtypeStruct((DEPTH, N_MOD, BATCH, d), F32),
        compiler_params=_params("arbitrary", "arbitrary"),
        name="adaln",
    )(c, ada_w, ada_b.reshape(DEPTH, 1, N_MOD * d))


def _row_spec():
    return pl.BlockSpec((None, ROW_TILE, D_MODEL), lambda b, l: (b, l, 0))


def _mod_spec(layer):
    return _layer_spec((N_MOD, BATCH, D_MODEL), layer)


def _mod_row(mod_ref, which):
    return mod_ref[which, pl.ds(pl.program_id(0), 1), :]


def _mlp_kernel(x_ref, mod_ref, g_ref, w1_ref, w2_ref, fg_ref, o_ref, *, layer, final):
    x = x_ref[...]
    h = _rmsnorm(x, g_ref[layer:layer + 1, :]) * (1.0 + _mod_row(mod_ref, SCALE2))
    h = (h + _mod_row(mod_ref, SHIFT2)).astype(BF16)
    acc = jnp.zeros((ROW_TILE, D_MODEL), F32)
    for k in range(D_FF // FF_TILE):
        hid = _dot(h, w1_ref[:, k * FF_TILE:(k + 1) * FF_TILE])
        hid = jnp.square(jnp.maximum(hid, 0.0)).astype(BF16)
        acc = acc + _dot(hid, w2_ref[k * FF_TILE:(k + 1) * FF_TILE, :])
    out = x + _mod_row(mod_ref, GATE2) * acc
    if final:
        out = _rmsnorm(out, fg_ref[...])
    o_ref[...] = out


def _mlp_layer(x, mods, norm_g, w1, w2, final_g, *, layer):
    d = D_MODEL
    return pl.pallas_call(
        functools.partial(_mlp_kernel, layer=layer, final=layer == DEPTH - 1),
        grid=(BATCH, SEQ // ROW_TILE),
        in_specs=[_row_spec(), _mod_spec(layer), _const_spec((DEPTH, d)),
                  _const_spec((d, D_FF)), _const_spec((D_FF, d)), _const_spec((1, d))],
        out_specs=_row_spec(),
        out_shape=jax.ShapeDtypeStruct((BATCH, SEQ, d), F32),
        compiler_params=_params("parallel", "arbitrary"),
        name="mlp",
    )(x, mods, norm_g, w1, w2, final_g)


def _conv_kernel(x_ref, mod_ref, g_ref, win_ref, cw_ref, cb_ref, wout_ref, *rest,
                 layer, idx, n_cast):
    casts, o_ref, (zbuf,) = _split_cast_refs(rest, n_cast)
    _cast_blocks(casts)
    d = D_MIX
    halo = SUBLANES

    @pl.when(pl.program_id(1) == 0)
    def _():
        zbuf[0:halo, :] = jnp.zeros((halo, d), F32)

    x = x_ref[...]
    h = _rmsnorm(x, g_ref[layer:layer + 1, :]) * (1.0 + _mod_row(mod_ref, SCALE1))
    h = (h + _mod_row(mod_ref, SHIFT1)).astype(BF16)
    bcx = _dot(h, win_ref[...])
    b_gate = bcx[:, 0:d]
    z = bcx[:, d:2 * d] * bcx[:, 2 * d:3 * d]
    zbuf[halo:halo + ROW_TILE, :] = z
    conv = (cw_ref[idx, 2:3, :] * z
            + cw_ref[idx, 1:2, :] * zbuf[halo - 1:halo - 1 + ROW_TILE, :]
            + cw_ref[idx, 0:1, :] * zbuf[halo - 2:halo - 2 + ROW_TILE, :]
            + cb_ref[idx:idx + 1, :])
    zbuf[0:halo, :] = zbuf[ROW_TILE:ROW_TILE + halo, :]
    y = _dot((b_gate * conv).astype(BF16), wout_ref[...])
    o_ref[...] = x + _mod_row(mod_ref, GATE1) * y


def _conv_layer(x, mods, norm_g, w_in, w_out, cw, cb, *, casts, layer, idx):
    d = D_MODEL
    n = cw.shape[0]
    grid = (BATCH, SEQ // ROW_TILE)
    c_in, c_out, c_shapes, c_args = _cast_plan(casts, grid[0] * grid[1], _tile_step)
    out, *cast_out = pl.pallas_call(
        functools.partial(_conv_kernel, layer=layer, idx=idx, n_cast=len(casts)),
        grid=grid,
        in_specs=[_row_spec(), _mod_spec(layer), _const_spec((DEPTH, d)),
                  _const_spec((d, 3 * D_MIX)), _const_spec((n, CONV_WIDTH, D_MIX)),
                  _const_spec((n, D_MIX)), _const_spec((D_MIX, d))] + c_in,
        out_specs=[_row_spec()] + c_out,
        out_shape=[jax.ShapeDtypeStruct((BATCH, SEQ, d), F32)] + c_shapes,
        scratch_shapes=[pltpu.VMEM((ROW_TILE + SUBLANES, D_MIX), F32)],
        compiler_params=_params("arbitrary", "arbitrary"),
        name="conv_mixer",
    )(x, mods, norm_g, w_in, cw, cb, w_out, *c_args)
    return out, cast_out


def _sg_kernel(x_ref, mod_ref, g_ref, win_ref, vg_ref, ws_ref, bs_ref, wout_ref, *rest,
               layer, idx, n_cast):
    casts, o_ref, (vm_ref,) = _split_cast_refs(rest, n_cast)
    _cast_blocks(casts)
    d = D_MIX
    x = x_ref[...]
    h = _rmsnorm(x, g_ref[layer:layer + 1, :]) * (1.0 + _mod_row(mod_ref, SCALE1))
    h = (h + _mod_row(mod_ref, SHIFT1)).astype(BF16)
    uv = _dot(h, win_ref[...])
    u = uv[:, 0:d]
    v = _rmsnorm(uv[:, d:2 * d], vg_ref[idx:idx + 1, :]).astype(BF16)
    row = lax.broadcasted_iota(jnp.int32, (CHUNK, CHUNK), 0)
    col = lax.broadcasted_iota(jnp.int32, (CHUNK, CHUNK), 1)
    causal = col <= row
    for hd in range(SG_HEADS):
        ws = jnp.where(causal, ws_ref[hd], 0.0).astype(BF16)
        bias = bs_ref[:, hd:hd + 1]
        lanes = slice(hd * SG_HEAD_DIM, (hd + 1) * SG_HEAD_DIM)
        for n in range(ROW_TILE // CHUNK):
            rows = slice(n * CHUNK, (n + 1) * CHUNK)
            vm_ref[rows, lanes] = _dot(ws, v[rows, lanes]) + bias
    y = _dot((u * vm_ref[...]).astype(BF16), wout_ref[...])
    o_ref[...] = x + _mod_row(mod_ref, GATE1) * y


def _sg_layer(x, mods, norm_g, w_in, w_out, v_g, w_s, b_s_t, *, casts, layer, idx):
    d = D_MODEL
    n = v_g.shape[0]
    grid = (BATCH, SEQ // ROW_TILE)
    c_in, c_out, c_shapes, c_args = _cast_plan(casts, grid[0] * grid[1], _tile_step)
    out, *cast_out = pl.pallas_call(
        functools.partial(_sg_kernel, layer=layer, idx=idx, n_cast=len(casts)),
        grid=grid,
        in_specs=[_row_spec(), _mod_spec(layer), _const_spec((DEPTH, d)),
                  _const_spec((d, 2 * D_MIX)), _const_spec((n, D_MIX)),
                  _layer_spec((SG_HEADS, CHUNK, CHUNK), idx), _layer_spec((CHUNK, SG_HEADS), idx),
                  _const_spec((D_MIX, d))] + c_in,
        out_specs=[_row_spec()] + c_out,
        out_shape=[jax.ShapeDtypeStruct((BATCH, SEQ, d), F32)] + c_shapes,
        scratch_shapes=[pltpu.VMEM((ROW_TILE, D_MIX), F32)],
        compiler_params=_params("arbitrary", "arbitrary"),
        name="sg_mixer",
    )(x, mods, norm_g, w_in, v_g, w_s, b_s_t, w_out, *c_args)
    return out, cast_out


def _replicate_block_diag(w, n_rep, row_block, col_block):
    rows = w.shape[0]
    src = lax.broadcasted_iota(jnp.int32, (col_block, n_rep * col_block), 0)
    dst = lax.broadcasted_iota(jnp.int32, (col_block, n_rep * col_block), 1)
    replicate = jnp.where(dst % col_block == src, 1.0, 0.0).astype(BF16)
    tiled = _dot(w.astype(BF16), replicate)
    r = lax.broadcasted_iota(jnp.int32, (rows, n_rep * col_block), 0)
    c = lax.broadcasted_iota(jnp.int32, (rows, n_rep * col_block), 1)
    return jnp.where(r // row_block == c // col_block, tiled, 0.0).astype(BF16)


def _s5_prep_kernel(are_ref, aim_ref, ldt_ref, bre_ref, bim_ref, cre_ref, cim_ref,
                    abar_re_ref, abar_im_ref, wbre_ref, wbim_ref, wcre_ref, wcim_ref):
    a_re = are_ref[...]
    a_im = aim_ref[...]
    dt = jnp.exp(ldt_ref[...])
    mag = jnp.exp(a_re * dt)
    abar_re = mag * jnp.cos(a_im * dt)
    abar_im = mag * jnp.sin(a_im * dt)
    den = a_re * a_re + a_im * a_im
    nr = abar_re - 1.0
    ni = abar_im
    f_re = ((nr * a_re + ni * a_im) / den)[:, None, :]
    f_im = ((ni * a_re - nr * a_im) / den)[:, None, :]
    b_re = bre_ref[...]
    b_im = bim_ref[...]
    abar_re_ref[...] = abar_re
    abar_im_ref[...] = abar_im
    bbar_re = (f_re * b_re - f_im * b_im).reshape(S5_GROUPS * S5_GROUP, S5_STATE)
    bbar_im = (f_re * b_im + f_im * b_re).reshape(S5_GROUPS * S5_GROUP, S5_STATE)
    for k in range(S5_NSUPER):
        rows_in = slice(k * MXU_DIM, (k + 1) * MXU_DIM)
        wbre_ref[k] = _replicate_block_diag(bbar_re[rows_in, :], S5_SUPER, S5_GROUP, S5_STATE)
        wbim_ref[k] = _replicate_block_diag(bbar_im[rows_in, :], S5_SUPER, S5_GROUP, S5_STATE)
        rows_out = slice(k * S5_SUPER_STATE, (k + 1) * S5_SUPER_STATE)
        wcre_ref[k] = _replicate_block_diag(cre_ref[rows_out, :], S5_SUPER, S5_STATE, S5_GROUP)
        wcim_ref[k] = _replicate_block_diag(cim_ref[rows_out, :], S5_SUPER, S5_STATE, S5_GROUP)


def _s5_prep(a_re, a_im, log_dt, b_re, b_im, c_re, c_im):
    g, p, h = S5_GROUPS, S5_STATE, S5_GROUP
    state_major = lambda c: c.transpose(0, 2, 1).reshape(g * p, h)
    return pl.pallas_call(
        _s5_prep_kernel,
        out_shape=[jax.ShapeDtypeStruct((g, p), F32), jax.ShapeDtypeStruct((g, p), F32),
                   jax.ShapeDtypeStruct((S5_NSUPER, MXU_DIM, S5_SUPER_STATE), BF16),
                   jax.ShapeDtypeStruct((S5_NSUPER, MXU_DIM, S5_SUPER_STATE), BF16),
                   jax.ShapeDtypeStruct((S5_NSUPER, S5_SUPER_STATE, MXU_DIM), BF16),
                   jax.ShapeDtypeStruct((S5_NSUPER, S5_SUPER_STATE, MXU_DIM), BF16)],
        name="s5_prep",
    )(a_re, a_im, log_dt.reshape(g, 1), b_re.transpose(0, 2, 1), b_im.transpose(0, 2, 1),
      state_major(c_re), state_major(c_im))


def _s5_kernel(x_ref, mod_ref, g_ref, win_ref, are_ref, aim_ref, wbre_ref, wbim_ref,
               wcre_ref, wcim_ref, dskip_ref, gluw_ref, glub_ref, wout_ref, *rest,
               layer, idx, n_cast):
    casts, o_ref, scratch = _split_cast_refs(rest, n_cast)
    bre_ref, bim_ref, state_re, state_im, y_ref = scratch
    _cast_blocks(casts)
    d = D_MODEL

    @pl.when(pl.program_id(0) == 0)
    def _():
        state_re[...] = jnp.zeros_like(state_re)
        state_im[...] = jnp.zeros_like(state_im)

    g = g_ref[layer:layer + 1, :]
    u_parts = []
    for part in range(S5_STEPS // S5_PERM_STEPS):
        steps = slice(part * S5_PERM_STEPS, (part + 1) * S5_PERM_STEPS)
        h_rows = []
        for b in range(BATCH):
            h_b = _rmsnorm(x_ref[b, steps, :], g) * (1.0 + mod_ref[SCALE1, b:b + 1, :])
            h_rows.append((h_b + mod_ref[SHIFT1, b:b + 1, :]).astype(BF16))
        h_part = jnp.concatenate(h_rows, axis=0)
        u_bm = _dot(h_part, win_ref[...]).reshape(BATCH, S5_PERM_STEPS, d)
        u_parts.append(jnp.swapaxes(u_bm, 0, 1).reshape(MXU_DIM, d))
    u = jnp.concatenate(u_parts, axis=0)
    ub = u.astype(BF16)
    block_ids = [(k, c0) for k in range(S5_NSUPER) for c0 in range(0, S5_SUPER_STATE, SCAN_COLS)]
    y_parts = [None] * S5_NSUPER
    for first in range(0, len(block_ids), SCAN_INFLIGHT):
        blocks = []
        for k, c0 in block_ids[first:first + SCAN_INFLIGHT]:
            uk = ub[:, k * MXU_DIM:(k + 1) * MXU_DIM]
            cols = slice(c0, c0 + SCAN_COLS)
            gcols = slice(k * S5_SUPER_STATE + c0, k * S5_SUPER_STATE + c0 + SCAN_COLS)
            bre_ref[:, cols] = _dot(uk, wbre_ref[k, :, cols])
            bim_ref[:, cols] = _dot(uk, wbim_ref[k, :, cols])
            blocks.append(dict(
                k=k, cols=cols, gcols=gcols,
                a_re=jnp.broadcast_to(are_ref[:, gcols], (BATCH, SCAN_COLS)),
                a_im=jnp.broadcast_to(aim_ref[:, gcols], (BATCH, SCAN_COLS)),
                s_re=state_re[:, gcols], s_im=state_im[:, gcols]))
        for t in range(S5_STEPS):
            rows = slice(t * BATCH, (t + 1) * BATCH)
            for blk in blocks:
                s_re, s_im, cols = blk["s_re"], blk["s_im"], blk["cols"]
                blk["s_re"] = blk["a_re"] * s_re - blk["a_im"] * s_im + bre_ref[rows, cols]
                blk["s_im"] = blk["a_re"] * s_im + blk["a_im"] * s_re + bim_ref[rows, cols]
                bre_ref[rows, cols] = blk["s_re"]
                bim_ref[rows, cols] = blk["s_im"]
        for blk in blocks:
            k, cols = blk["k"], blk["cols"]
            state_re[:, blk["gcols"]] = blk["s_re"]
            state_im[:, blk["gcols"]] = blk["s_im"]
            part = (_dot(bre_ref[:, cols].astype(BF16), wcre_ref[k, cols, :])
                    - _dot(bim_ref[:, cols].astype(BF16), wcim_ref[k, cols, :]))
            y_parts[k] = part if y_parts[k] is None else y_parts[k] + part
    for k in range(S5_NSUPER):
        y_ref[:, k * MXU_DIM:(k + 1) * MXU_DIM] = y_parts[k]

    y = jax.nn.gelu(y_ref[...] + dskip_ref[idx:idx + 1, :] * u)
    y = y * jax.nn.sigmoid(_dot(y.astype(BF16), gluw_ref[...]) + glub_ref[idx:idx + 1, :])

    for part in range(S5_STEPS // S5_PERM_STEPS):
        steps = slice(part * S5_PERM_STEPS, (part + 1) * S5_PERM_STEPS)
        y_tm = y[part * MXU_DIM:(part + 1) * MXU_DIM, :].reshape(S5_PERM_STEPS, BATCH, d)
        y_part = jnp.swapaxes(y_tm, 0, 1).reshape(MXU_DIM, d).astype(BF16)
        out = _dot(y_part, wout_ref[...])
        for b in range(BATCH):
            out_b = out[b * S5_PERM_STEPS:(b + 1) * S5_PERM_STEPS, :]
            o_ref[b, steps, :] = x_ref[b, steps, :] + mod_ref[GATE1, b:b + 1, :] * out_b


def _s5_layer(x, mods, norm_g, w_in, glu_w, w_out, abar_re, abar_im, wb_re, wb_im, wc_re, wc_im,
              d_skip, glu_b, *, casts, layer, idx):
    d = D_MODEL
    n = d_skip.shape[0]
    n_steps = SEQ // S5_STEPS
    x_spec = pl.BlockSpec((BATCH, S5_STEPS, d), lambda i: (0, i, 0))
    c_in, c_out, c_shapes, c_args = _cast_plan(casts, n_steps, lambda i: i)
    out, *cast_out = pl.pallas_call(
        functools.partial(_s5_kernel, layer=layer, idx=idx, n_cast=len(casts)),
        grid=(n_steps,),
        in_specs=[x_spec, _mod_spec(layer), _const_spec((DEPTH, d)),
                  _const_spec((d, D_MIX)),
                  _const_spec((1, S5_NSTATE)), _const_spec((1, S5_NSTATE)),
                  _const_spec((S5_NSUPER, MXU_DIM, S5_SUPER_STATE)),
                  _const_spec((S5_NSUPER, MXU_DIM, S5_SUPER_STATE)),
                  _const_spec((S5_NSUPER, S5_SUPER_STATE, MXU_DIM)),
                  _const_spec((S5_NSUPER, S5_SUPER_STATE, MXU_DIM)),
                  _const_spec((n, D_MIX)), _const_spec((D_MIX, D_MIX)),
                  _const_spec((n, D_MIX)), _const_spec((D_MIX, d))] + c_in,
        out_specs=[x_spec] + c_out,
        out_shape=[jax.ShapeDtypeStruct((BATCH, SEQ, d), F32)] + c_shapes,
        scratch_shapes=[pltpu.VMEM((S5_ROWS, S5_SUPER_STATE), F32),
                        pltpu.VMEM((S5_ROWS, S5_SUPER_STATE), F32),
                        pltpu.VMEM((BATCH, S5_NSTATE), F32),
                        pltpu.VMEM((BATCH, S5_NSTATE), F32),
                        pltpu.VMEM((S5_ROWS, D_MIX), F32)],
        compiler_params=_params("arbitrary"),
        name="s5_mixer",
    )(x, mods, norm_g, w_in, abar_re, abar_im, wb_re, wb_im, wc_re, wc_im, d_skip,
      glu_w, glu_b, w_out, *c_args)
    return out, cast_out


def kernel(x, c, ada_w, ada_b, norm1_g, norm2_g, ff_w1, ff_w2, final_g, conv_w_in, conv_w, conv_b, conv_w_out, ssm_w_in, ssm_a_re, ssm_a_im, ssm_log_dt, ssm_b_re, ssm_b_im, ssm_c_re, ssm_c_im, ssm_d, ssm_glu_w, ssm_glu_b, ssm_w_out, sg_w_in, sg_v_g, sg_w_s, sg_b_s, sg_w_out):
    mods = _adaln(c, ada_w, ada_b)
    sg_b_s_t = sg_b_s.transpose(0, 2, 1)
    final_g = final_g.reshape(1, D_MODEL)

    def mixer_weights(i):
        j = i // N_MIXERS
        stacks = [(conv_w_in, conv_w_out), (ssm_w_in, ssm_glu_w, ssm_w_out), (sg_w_in, sg_w_out)]
        return [(w, j) for w in stacks[i % N_MIXERS]]

    w_mix = [w[j].astype(BF16) for w, j in mixer_weights(0)]
    for i in range(DEPTH):
        kind = i % N_MIXERS
        j = i // N_MIXERS
        casts = [(ff_w1, i), (ff_w2, i)] + (mixer_weights(i + 1) if i + 1 < DEPTH else [])
        if kind == 0:
            x, cast_out = _conv_layer(x, mods, norm1_g, *w_mix, conv_w, conv_b,
                                      casts=casts, layer=i, idx=j)
        elif kind == 1:
            abar_re, abar_im, wb_re, wb_im, wc_re, wc_im = _s5_prep(
                ssm_a_re[j], ssm_a_im[j], ssm_log_dt[j], ssm_b_re[j], ssm_b_im[j],
                ssm_c_re[j], ssm_c_im[j])
            x, cast_out = _s5_layer(x, mods, norm1_g, *w_mix,
                                    abar_re.reshape(1, S5_NSTATE), abar_im.reshape(1, S5_NSTATE),
                                    wb_re, wb_im, wc_re, wc_im, ssm_d, ssm_glu_b,
                                    casts=casts, layer=i, idx=j)
        else:
            x, cast_out = _sg_layer(x, mods, norm1_g, *w_mix, sg_v_g, sg_w_s, sg_b_s_t,
                                    casts=casts, layer=i, idx=j)
        w1, w2, *w_mix = cast_out
        x = _mlp_layer(x, mods, norm2_g, w1, w2, final_g, layer=i)
    return x
```

```python
import functools

import jax
import jax.numpy as jnp
from jax import lax
from jax.experimental import pallas as pl
from jax.experimental.pallas import tpu as pltpu

D_MODEL = 1024
BATCH = 8
SEQ = 4096
DEPTH = 4
N_MIXERS = 3
D_FF = 4 * D_MODEL
D_MIX = D_MODEL
EPS = 1e-6
CONV_WIDTH = 3
S5_GROUP = 16
S5_GROUPS = D_MIX // S5_GROUP
S5_STATE = 64
CHUNK = 128
SG_HEADS = 8
SG_HEAD_DIM = D_MIX // SG_HEADS

SUBLANES = 8
MXU_DIM = 256

S5_SUPER = MXU_DIM // S5_GROUP
S5_NSUPER = S5_GROUPS // S5_SUPER
S5_SUPER_STATE = S5_SUPER * S5_STATE
S5_NSTATE = S5_GROUPS * S5_STATE

ROW_TILE = 1024
S5_ROWS = 1024
S5_STEPS = S5_ROWS // BATCH
S5_PERM_STEPS = MXU_DIM // BATCH
FF_TILE = 1024
SCAN_COLS = 256
SCAN_INFLIGHT = 2
assert SCAN_INFLIGHT <= S5_SUPER_STATE // SCAN_COLS
VMEM_LIMIT = 60 * 1024 * 1024

N_MOD = 6
SHIFT1, SCALE1, GATE1, SHIFT2, SCALE2, GATE2 = range(N_MOD)

F32 = jnp.float32
BF16 = jnp.bfloat16


def _dot(a, b):
    return jnp.dot(a, b, preferred_element_type=F32)


def _rmsnorm(x, g):
    y = x * lax.rsqrt(jnp.mean(x * x, axis=-1, keepdims=True) + EPS)
    return y * g


def _const_spec(shape):
    zeros = (0,) * len(shape)
    return pl.BlockSpec(shape, lambda *_: zeros, pipeline_mode=pl.Buffered(1))


def _layer_spec(shape, layer):
    zeros = (0,) * len(shape)
    return pl.BlockSpec((None,) + tuple(shape), lambda *_: (layer,) + zeros,
                        pipeline_mode=pl.Buffered(1))


def _params(*semantics):
    return pltpu.CompilerParams(dimension_semantics=semantics, vmem_limit_bytes=VMEM_LIMIT)


BF16_SUBLANES = 16


def _cast_plan(casts, n_steps, step_index):
    in_specs, out_specs, out_shapes, args = [], [], [], []
    for stacked, entry in casts:
        _, rows, cols = stacked.shape
        block_rows, rem = divmod(rows, n_steps)
        assert rem == 0 and block_rows % BF16_SUBLANES == 0, (rows, n_steps)
        in_specs.append(pl.BlockSpec((None, block_rows, cols),
                                     lambda *g, entry=entry: (entry, step_index(*g), 0)))
        out_specs.append(pl.BlockSpec((block_rows, cols), lambda *g: (step_index(*g), 0)))
        out_shapes.append(jax.ShapeDtypeStruct((rows, cols), BF16))
        args.append(stacked)
    return in_specs, out_specs, out_shapes, args


def _split_cast_refs(rest, n_cast):
    cast_in = rest[:n_cast]
    o_ref = rest[n_cast]
    cast_out = rest[n_cast + 1:2 * n_cast + 1]
    return list(zip(cast_in, cast_out)), o_ref, rest[2 * n_cast + 1:]


def _cast_blocks(casts):
    for src, dst in casts:
        dst[...] = src[...].astype(BF16)


def _tile_step(b, l):
    return b * (SEQ // ROW_TILE) + l


ADALN_ROWS = 3


def _adaln_kernel(c_ref, w_ref, b_ref, o_ref):
    d = D_MODEL
    c = c_ref[...]
    c_act = (c * jax.nn.sigmoid(c)).astype(BF16)
    mod = _dot(c_act, w_ref[...].astype(BF16)) + b_ref[...]
    for j in range(ADALN_ROWS):
        o_ref[j] = mod[:, j * d:(j + 1) * d]


def _adaln(c, ada_w, ada_b):
    d = D_MODEL
    return pl.pallas_call(
        _adaln_kernel,
        grid=(DEPTH, N_MOD // ADALN_ROWS),
        in_specs=[
            _const_spec((BATCH, d)),
            pl.BlockSpec((None, d, ADALN_ROWS * d), lambda i, j: (i, 0, j)),
            pl.BlockSpec((None, 1, ADALN_ROWS * d), lambda i, j: (i, 0, j)),
        ],
        out_specs=pl.BlockSpec((None, ADALN_ROWS, BATCH, d), lambda i, j: (i, j, 0, 0)),
        out_shape=jax.ShapeDtypeStruct((DEPTH, N_MOD, BATCH, d), F32),
        compiler_params=_params("arbitrary", "arbitrary"),
        name="adaln",
    )(c, ada_w, ada_b.reshape(DEPTH, 1, N_MOD * d))


def _row_spec():
    return pl.BlockSpec((None, ROW_TILE, D_MODEL), lambda b, l: (b, l, 0))


def _mod_spec(layer):
    return _layer_spec((N_MOD, BATCH, D_MODEL), layer)


def _mod_row(mod_ref, which):
    return mod_ref[which, pl.ds(pl.program_id(0), 1), :]


def _mlp_kernel(x_ref, mod_ref, g_ref, w1_ref, w2_ref, fg_ref, o_ref, *, layer, final):
    x = x_ref[...]
    h = _rmsnorm(x, g_ref[layer:layer + 1, :]) * (1.0 + _mod_row(mod_ref, SCALE2))
    h = (h + _mod_row(mod_ref, SHIFT2)).astype(BF16)
    acc = jnp.zeros((ROW_TILE, D_MODEL), F32)
    for k in range(D_FF // FF_TILE):
        hid = _dot(h, w1_ref[:, k * FF_TILE:(k + 1) * FF_TILE])
        hid = jnp.square(jnp.maximum(hid, 0.0)).astype(BF16)
        acc = acc + _dot(hid, w2_ref[k * FF_TILE:(k + 1) * FF_TILE, :])
    out = x + _mod_row(mod_ref, GATE2) * acc
    if final:
        out = _rmsnorm(out, fg_ref[...])
    o_ref[...] = out


def _mlp_layer(x, mods, norm_g, w1, w2, final_g, *, layer):
    d = D_MODEL
    return pl.pallas_call(
        functools.partial(_mlp_kernel, layer=layer, final=layer == DEPTH - 1),
        grid=(BATCH, SEQ // ROW_TILE),
        in_specs=[_row_spec(), _mod_spec(layer), _const_spec((DEPTH, d)),
                  _const_spec((d, D_FF)), _const_spec((D_FF, d)), _const_spec((1, d))],
        out_specs=_row_spec(),
        out_shape=jax.ShapeDtypeStruct((BATCH, SEQ, d), F32),
        compiler_params=_params("parallel", "arbitrary"),
        name="mlp",
    )(x, mods, norm_g, w1, w2, final_g)


def _conv_kernel(x_ref, mod_ref, g_ref, win_ref, cw_ref, cb_ref, wout_ref, *rest,
                 layer, idx, n_cast):
    casts, o_ref, (zbuf,) = _split_cast_refs(rest, n_cast)
    _cast_blocks(casts)
    d = D_MIX
    halo = SUBLANES

    @pl.when(pl.program_id(1) == 0)
    def _():
        zbuf[0:halo, :] = jnp.zeros((halo, d), F32)

    x = x_ref[...]
    h = _rmsnorm(x, g_ref[layer:layer + 1, :]) * (1.0 + _mod_row(mod_ref, SCALE1))
    h = (h + _mod_row(mod_ref, SHIFT1)).astype(BF16)
    bcx = _dot(h, win_ref[...])
    b_gate = bcx[:, 0:d]
    z = bcx[:, d:2 * d] * bcx[:, 2 * d:3 * d]
    zbuf[halo:halo + ROW_TILE, :] = z
    conv = (cw_ref[idx, 2:3, :] * z
            + cw_ref[idx, 1:2, :] * zbuf[halo - 1:halo - 1 + ROW_TILE, :]
            + cw_ref[idx, 0:1, :] * zbuf[halo - 2:halo - 2 + ROW_TILE, :]
            + cb_ref[idx:idx + 1, :])
    zbuf[0:halo, :] = zbuf[ROW_TILE:ROW_TILE + halo, :]
    y = _dot((b_gate * conv).astype(BF16), wout_ref[...])
    o_ref[...] = x + _mod_row(mod_ref, GATE1) * y


def _conv_layer(x, mods, norm_g, w_in, w_out, cw, cb, *, casts, layer, idx):
    d = D_MODEL
    n = cw.shape[0]
    grid = (BATCH, SEQ // ROW_TILE)
    c_in, c_out, c_shapes, c_args = _cast_plan(casts, grid[0] * grid[1], _tile_step)
    out, *cast_out = pl.pallas_call(
        functools.partial(_conv_kernel, layer=layer, idx=idx, n_cast=len(casts)),
        grid=grid,
        in_specs=[_row_spec(), _mod_spec(layer), _const_spec((DEPTH, d)),
                  _const_spec((d, 3 * D_MIX)), _const_spec((n, CONV_WIDTH, D_MIX)),
                  _const_spec((n, D_MIX)), _const_spec((D_MIX, d))] + c_in,
        out_specs=[_row_spec()] + c_out,
        out_shape=[jax.ShapeDtypeStruct((BATCH, SEQ, d), F32)] + c_shapes,
        scratch_shapes=[pltpu.VMEM((ROW_TILE + SUBLANES, D_MIX), F32)],
        compiler_params=_params("arbitrary", "arbitrary"),
        name="conv_mixer",
    )(x, mods, norm_g, w_in, cw, cb, w_out, *c_args)
    return out, cast_out


def _sg_mlp_kernel(x_ref, mod_ref, g1_ref, g2_ref, win_ref, vg_ref, ws_ref, bs_ref, wout_ref,
                   w1_ref, w2_ref, *rest, layer, idx, n_cast):
    casts, o_ref, (vm_ref,) = _split_cast_refs(rest, n_cast)
    _cast_blocks(casts)
    d = D_MIX
    x = x_ref[...]
    h = _rmsnorm(x, g1_ref[layer:layer + 1, :]) * (1.0 + _mod_row(mod_ref, SCALE1))
    h = (h + _mod_row(mod_ref, SHIFT1)).astype(BF16)
    uv = _dot(h, win_ref[...])
    u = uv[:, 0:d]
    v = _rmsnorm(uv[:, d:2 * d], vg_ref[idx:idx + 1, :]).astype(BF16)
    row = lax.broadcasted_iota(jnp.int32, (CHUNK, CHUNK), 0)
    col = lax.broadcasted_iota(jnp.int32, (CHUNK, CHUNK), 1)
    causal = col <= row
    for hd in range(SG_HEADS):
        ws = jnp.where(causal, ws_ref[hd], 0.0).astype(BF16)
        bias = bs_ref[:, hd:hd + 1]
        lanes = slice(hd * SG_HEAD_DIM, (hd + 1) * SG_HEAD_DIM)
        for n in range(ROW_TILE // CHUNK):
            rows = slice(n * CHUNK, (n + 1) * CHUNK)
            vm_ref[rows, lanes] = _dot(ws, v[rows, lanes]) + bias
    y = _dot((u * vm_ref[...]).astype(BF16), wout_ref[...])
    x = x + _mod_row(mod_ref, GATE1) * y

    h = _rmsnorm(x, g2_ref[layer:layer + 1, :]) * (1.0 + _mod_row(mod_ref, SCALE2))
    h = (h + _mod_row(mod_ref, SHIFT2)).astype(BF16)
    acc = jnp.zeros((ROW_TILE, D_MODEL), F32)
    for k in range(D_FF // FF_TILE):
        hid = _dot(h, w1_ref[:, k * FF_TILE:(k + 1) * FF_TILE])
        hid = jnp.square(jnp.maximum(hid, 0.0)).astype(BF16)
        acc = acc + _dot(hid, w2_ref[k * FF_TILE:(k + 1) * FF_TILE, :])
    o_ref[...] = x + _mod_row(mod_ref, GATE2) * acc


def _sg_mlp_layer(x, mods, norm1_g, norm2_g, w_in, w_out, v_g, w_s, b_s_t, w1, w2, *,
                  casts, layer, idx):
    d = D_MODEL
    n = v_g.shape[0]
    grid = (BATCH, SEQ // ROW_TILE)
    c_in, c_out, c_shapes, c_args = _cast_plan(casts, grid[0] * grid[1], _tile_step)
    out, *cast_out = pl.pallas_call(
        functools.partial(_sg_mlp_kernel, layer=layer, idx=idx, n_cast=len(casts)),
        grid=grid,
        in_specs=[_row_spec(), _mod_spec(layer), _const_spec((DEPTH, d)), _const_spec((DEPTH, d)),
                  _const_spec((d, 2 * D_MIX)), _const_spec((n, D_MIX)),
                  _layer_spec((SG_HEADS, CHUNK, CHUNK), idx), _layer_spec((CHUNK, SG_HEADS), idx),
                  _const_spec((D_MIX, d)), _const_spec((d, D_FF)), _const_spec((D_FF, d))] + c_in,
        out_specs=[_row_spec()] + c_out,
        out_shape=[jax.ShapeDtypeStruct((BATCH, SEQ, d), F32)] + c_shapes,
        scratch_shapes=[pltpu.VMEM((ROW_TILE, D_MIX), F32)],
        compiler_params=_params("arbitrary", "arbitrary"),
        name="sg_mixer_mlp",
    )(x, mods, norm1_g, norm2_g, w_in, v_g, w_s, b_s_t, w_out, w1, w2, *c_args)
    return out, cast_out


def _replicate_block_diag(w, n_rep, row_block, col_block):
    rows = w.shape[0]
    src = lax.broadcasted_iota(jnp.int32, (col_block, n_rep * col_block), 0)
    dst = lax.broadcasted_iota(jnp.int32, (col_block, n_rep * col_block), 1)
    replicate = jnp.where(dst % col_block == src, 1.0, 0.0).astype(BF16)
    tiled = _dot(w.astype(BF16), replicate)
    r = lax.broadcasted_iota(jnp.int32, (rows, n_rep * col_block), 0)
    c = lax.broadcasted_iota(jnp.int32, (rows, n_rep * col_block), 1)
    return jnp.where(r // row_block == c // col_block, tiled, 0.0).astype(BF16)


def _s5_prep_kernel(are_ref, aim_ref, ldt_ref, bre_ref, bim_ref, cre_ref, cim_ref,
                    abar_re_ref, abar_im_ref, wbre_ref, wbim_ref, wcre_ref, wcim_ref):
    a_re = are_ref[...]
    a_im = aim_ref[...]
    dt = jnp.exp(ldt_ref[...])
    mag = jnp.exp(a_re * dt)
    abar_re = mag * jnp.cos(a_im * dt)
    abar_im = mag * jnp.sin(a_im * dt)
    den = a_re * a_re + a_im * a_im
    nr = abar_re - 1.0
    ni = abar_im
    f_re = ((nr * a_re + ni * a_im) / den)[:, None, :]
    f_im = ((ni * a_re - nr * a_im) / den)[:, None, :]
    b_re = bre_ref[...]
    b_im = bim_ref[...]
    abar_re_ref[...] = abar_re
    abar_im_ref[...] = abar_im
    bbar_re = (f_re * b_re - f_im * b_im).reshape(S5_GROUPS * S5_GROUP, S5_STATE)
    bbar_im = (f_re * b_im + f_im * b_re).reshape(S5_GROUPS * S5_GROUP, S5_STATE)
    for k in range(S5_NSUPER):
        rows_in = slice(k * MXU_DIM, (k + 1) * MXU_DIM)
        wbre_ref[k] = _replicate_block_diag(bbar_re[rows_in, :], S5_SUPER, S5_GROUP, S5_STATE)
        wbim_ref[k] = _replicate_block_diag(bbar_im[rows_in, :], S5_SUPER, S5_GROUP, S5_STATE)
        rows_out = slice(k * S5_SUPER_STATE, (k + 1) * S5_SUPER_STATE)
        wcre_ref[k] = _replicate_block_diag(cre_ref[rows_out, :], S5_SUPER, S5_STATE, S5_GROUP)
        wcim_ref[k] = _replicate_block_diag(cim_ref[rows_out, :], S5_SUPER, S5_STATE, S5_GROUP)


def _s5_prep(a_re, a_im, log_dt, b_re, b_im, c_re, c_im):
    g, p, h = S5_GROUPS, S5_STATE, S5_GROUP
    state_major = lambda c: c.transpose(0, 2, 1).reshape(g * p, h)
    return pl.pallas_call(
        _s5_prep_kernel,
        out_shape=[jax.ShapeDtypeStruct((g, p), F32), jax.ShapeDtypeStruct((g, p), F32),
                   jax.ShapeDtypeStruct((S5_NSUPER, MXU_DIM, S5_SUPER_STATE), BF16),
                   jax.ShapeDtypeStruct((S5_NSUPER, MXU_DIM, S5_SUPER_STATE), BF16),
                   jax.ShapeDtypeStruct((S5_NSUPER, S5_SUPER_STATE, MXU_DIM), BF16),
                   jax.ShapeDtypeStruct((S5_NSUPER, S5_SUPER_STATE, MXU_DIM), BF16)],
        name="s5_prep",
    )(a_re, a_im, log_dt.reshape(g, 1), b_re.transpose(0, 2, 1), b_im.transpose(0, 2, 1),
      state_major(c_re), state_major(c_im))


def _s5_kernel(x_ref, mod_ref, g_ref, win_ref, are_ref, aim_ref, wbre_ref, wbim_ref,
               wcre_ref, wcim_ref, dskip_ref, gluw_ref, glub_ref, wout_ref, *rest,
               layer, idx, n_cast):
    casts, o_ref, scratch = _split_cast_refs(rest, n_cast)
    bre_ref, bim_ref, state_re, state_im, y_ref = scratch
    _cast_blocks(casts)
    d = D_MODEL

    @pl.when(pl.program_id(0) == 0)
    def _():
        state_re[...] = jnp.zeros_like(state_re)
        state_im[...] = jnp.zeros_like(state_im)

    g = g_ref[layer:layer + 1, :]
    u_parts = []
    for part in range(S5_STEPS // S5_PERM_STEPS):
        steps = slice(part * S5_PERM_STEPS, (part + 1) * S5_PERM_STEPS)
        h_rows = []
        for b in range(BATCH):
            h_b = _rmsnorm(x_ref[b, steps, :], g) * (1.0 + mod_ref[SCALE1, b:b + 1, :])
            h_rows.append((h_b + mod_ref[SHIFT1, b:b + 1, :]).astype(BF16))
        h_part = jnp.concatenate(h_rows, axis=0)
        u_bm = _dot(h_part, win_ref[...]).reshape(BATCH, S5_PERM_STEPS, d)
        u_parts.append(jnp.swapaxes(u_bm, 0, 1).reshape(MXU_DIM, d))
    u = jnp.concatenate(u_parts, axis=0)
    ub = u.astype(BF16)
    block_ids = [(k, c0) for k in range(S5_NSUPER) for c0 in range(0, S5_SUPER_STATE, SCAN_COLS)]
    y_parts = [None] * S5_NSUPER
    for first in range(0, len(block_ids), SCAN_INFLIGHT):
        blocks = []
        for k, c0 in block_ids[first:first + SCAN_INFLIGHT]:
            uk = ub[:, k * MXU_DIM:(k + 1) * MXU_DIM]
            cols = slice(c0, c0 + SCAN_COLS)
            gcols = slice(k * S5_SUPER_STATE + c0, k * S5_SUPER_STATE + c0 + SCAN_COLS)
            bre_ref[:, cols] = _dot(uk, wbre_ref[k, :, cols])
            bim_ref[:, cols] = _dot(uk, wbim_ref[k, :, cols])
            blocks.append(dict(
                k=k, cols=cols, gcols=gcols,
                a_re=jnp.broadcast_to(are_ref[:, gcols], (BATCH, SCAN_COLS)),
                a_im=jnp.broadcast_to(aim_ref[:, gcols], (BATCH, SCAN_COLS)),
                s_re=state_re[:, gcols], s_im=state_im[:, gcols]))
        for t in range(S5_STEPS):
            rows = slice(t * BATCH, (t + 1) * BATCH)
            for blk in blocks:
                s_re, s_im, cols = blk["s_re"], blk["s_im"], blk["cols"]
                blk["s_re"] = blk["a_re"] * s_re - blk["a_im"] * s_im + bre_ref[rows, cols]
                blk["s_im"] = blk["a_re"] * s_im + blk["a_im"] * s_re + bim_ref[rows, cols]
                bre_ref[rows, cols] = blk["s_re"]
                bim_ref[rows, cols] = blk["s_im"]
        for blk in blocks:
            k, cols = blk["k"], blk["cols"]
            state_re[:, blk["gcols"]] = blk["s_re"]
            state_im[:, blk["gcols"]] = blk["s_im"]
            part = (_dot(bre_ref[:, cols].astype(BF16), wcre_ref[k, cols, :])
                    - _dot(bim_ref[:, cols].astype(BF16), wcim_ref[k, cols, :]))
            y_parts[k] = part if y_parts[k] is None else y_parts[k] + part
    for k in range(S5_NSUPER):
        y_ref[:, k * MXU_DIM:(k + 1) * MXU_DIM] = y_parts[k]

    y = jax.nn.gelu(y_ref[...] + dskip_ref[idx:idx + 1, :] * u)
    y = y * jax.nn.sigmoid(_dot(y.astype(BF16), gluw_ref[...]) + glub_ref[idx:idx + 1, :])

    for part in range(S5_STEPS // S5_PERM_STEPS):
        steps = slice(part * S5_PERM_STEPS, (part + 1) * S5_PERM_STEPS)
        y_tm = y[part * MXU_DIM:(part + 1) * MXU_DIM, :].reshape(S5_PERM_STEPS, BATCH, d)
        y_part = jnp.swapaxes(y_tm, 0, 1).reshape(MXU_DIM, d).astype(BF16)
        out = _dot(y_part, wout_ref[...])
        for b in range(BATCH):
            out_b = out[b * S5_PERM_STEPS:(b + 1) * S5_PERM_STEPS, :]
            o_ref[b, steps, :] = x_ref[b, steps, :] + mod_ref[GATE1, b:b + 1, :] * out_b


def _s5_layer(x, mods, norm_g, w_in, glu_w, w_out, abar_re, abar_im, wb_re, wb_im, wc_re, wc_im,
              d_skip, glu_b, *, casts, layer, idx):
    d = D_MODEL
    n = d_skip.shape[0]
    n_steps = SEQ // S5_STEPS
    x_spec = pl.BlockSpec((BATCH, S5_STEPS, d), lambda i: (0, i, 0))
    c_in, c_out, c_shapes, c_args = _cast_plan(casts, n_steps, lambda i: i)
    out, *cast_out = pl.pallas_call(
        functools.partial(_s5_kernel, layer=layer, idx=idx, n_cast=len(casts)),
        grid=(n_steps,),
        in_specs=[x_spec, _mod_spec(layer), _const_spec((DEPTH, d)),
                  _const_spec((d, D_MIX)),
                  _const_spec((1, S5_NSTATE)), _const_spec((1, S5_NSTATE)),
                  _const_spec((S5_NSUPER, MXU_DIM, S5_SUPER_STATE)),
                  _const_spec((S5_NSUPER, MXU_DIM, S5_SUPER_STATE)),
                  _const_spec((S5_NSUPER, S5_SUPER_STATE, MXU_DIM)),
                  _const_spec((S5_NSUPER, S5_SUPER_STATE, MXU_DIM)),
                  _const_spec((n, D_MIX)), _const_spec((D_MIX, D_MIX)),
                  _const_spec((n, D_MIX)), _const_spec((D_MIX, d))] + c_in,
        out_specs=[x_spec] + c_out,
        out_shape=[jax.ShapeDtypeStruct((BATCH, SEQ, d), F32)] + c_shapes,
        scratch_shapes=[pltpu.VMEM((S5_ROWS, S5_SUPER_STATE), F32),
                        pltpu.VMEM((S5_ROWS, S5_SUPER_STATE), F32),
                        pltpu.VMEM((BATCH, S5_NSTATE), F32),
                        pltpu.VMEM((BATCH, S5_NSTATE), F32),
                        pltpu.VMEM((S5_ROWS, D_MIX), F32)],
        compiler_params=_params("arbitrary"),
        name="s5_mixer",
    )(x, mods, norm_g, w_in, abar_re, abar_im, wb_re, wb_im, wc_re, wc_im, d_skip,
      glu_w, glu_b, w_out, *c_args)
    return out, cast_out


def kernel(x, c, ada_w, ada_b, norm1_g, norm2_g, ff_w1, ff_w2, final_g, conv_w_in, conv_w, conv_b, conv_w_out, ssm_w_in, ssm_a_re, ssm_a_im, ssm_log_dt, ssm_b_re, ssm_b_im, ssm_c_re, ssm_c_im, ssm_d, ssm_glu_w, ssm_glu_b, ssm_w_out, sg_w_in, sg_v_g, sg_w_s, sg_b_s, sg_w_out):
    mods = _adaln(c, ada_w, ada_b)
    sg_b_s_t = sg_b_s.transpose(0, 2, 1)
    final_g = final_g.reshape(1, D_MODEL)

    def mixer_weights(i):
        j = i // N_MIXERS
        stacks = [(conv_w_in, conv_w_out), (ssm_w_in, ssm_glu_w, ssm_w_out), (sg_w_in, sg_w_out)]
        return [(w, j) for w in stacks[i % N_MIXERS]]

    fused = lambda i: i % N_MIXERS == 2
    w_mix = [w[j].astype(BF16) for w, j in mixer_weights(0)]
    w_ff = None
    for i in range(DEPTH):
        kind = i % N_MIXERS
        j = i // N_MIXERS
        has_next = i + 1 < DEPTH
        casts = [] if fused(i) else [(ff_w1, i), (ff_w2, i)]
        if has_next:
            casts = casts + mixer_weights(i + 1)
            if fused(i + 1):
                casts = casts + [(ff_w1, i + 1), (ff_w2, i + 1)]
        if kind == 0:
            x, cast_out = _conv_layer(x, mods, norm1_g, *w_mix, conv_w, conv_b,
                                      casts=casts, layer=i, idx=j)
        elif kind == 1:
            abar_re, abar_im, wb_re, wb_im, wc_re, wc_im = _s5_prep(
                ssm_a_re[j], ssm_a_im[j], ssm_log_dt[j], ssm_b_re[j], ssm_b_im[j],
                ssm_c_re[j], ssm_c_im[j])
            x, cast_out = _s5_layer(x, mods, norm1_g, *w_mix,
                                    abar_re.reshape(1, S5_NSTATE), abar_im.reshape(1, S5_NSTATE),
                                    wb_re, wb_im, wc_re, wc_im, ssm_d, ssm_glu_b,
                                    casts=casts, layer=i, idx=j)
        else:
            x, w_mix = _sg_mlp_layer(x, mods, norm1_g, norm2_g, *w_mix, sg_v_g, sg_w_s, sg_b_s_t,
                                     *w_ff, casts=casts, layer=i, idx=j)
            continue
        w1, w2, *w_mix = cast_out
        if has_next and fused(i + 1):
            w_mix, w_ff = w_mix[:-2], w_mix[-2:]
        x = _mlp_layer(x, mods, norm2_g, w1, w2, final_g, layer=i)
    return x
```
